```python
import jax, jax.numpy as jnp
from jax import lax
import numpy as np

D_MODEL = 1024
BATCH = 16
SEQ = 2048
DEPTH = 2
DEC_BATCH = 128
DEC_SEQ = 8
PAST_LEN = 16384
PAGE_SIZE = 128

N_A_LAYERS = DEPTH // 2
N_B_LAYERS = DEPTH - N_A_LAYERS
CONV_W = 3
N_HEADS = 8
NOPE_DIM = 128
ROPE_DIM = 64
V_DIM = 128
KV_RANK = 256
Q_RANK = 768
ROPE_THETA = 10000.0
SM_SCALE = (NOPE_DIM + ROPE_DIM) ** -0.5
Q_BLOCK = 128
N_EXPERTS = 32
TOP_K = 4
D_FF = D_MODEL
SWIGLU_LIMIT = 7.0
SWIGLU_ALPHA = 1.702
MOE_BLOCK = 128
RMS_EPS = 1e-6
NEG_INF = -1e30

kernel_name = 'yoco_shortconv_mla_moe_step'


def rms_norm(x, g):
    x32 = x.astype(jnp.float32)
    y = x32 * lax.rsqrt(jnp.mean(x32 * x32, axis=-1, keepdims=True) + RMS_EPS)
    return (y * g.astype(jnp.float32)).astype(x.dtype)


def modulate(x, shift, scale):
    return x * (1 + scale[:, None, :]) + shift[:, None, :]


def rope(x, pos):
    half = ROPE_DIM // 2
    inv = jnp.exp(-jnp.log(ROPE_THETA) * jnp.arange(half, dtype=jnp.float32) / half)
    ang = pos.astype(jnp.float32)[:, None] * inv[None, :]
    cos = jnp.cos(ang)[None, :, None, :]
    sin = jnp.sin(ang)[None, :, None, :]
    x32 = x.astype(jnp.float32)
    x1, x2 = x32[..., :half], x32[..., half:]
    return jnp.concatenate([x1 * cos - x2 * sin, x1 * sin + x2 * cos], axis=-1).astype(x.dtype)


def short_conv_mixer(x, u_prev, w_in, w_dw, w_out):
    b_gate, c_gate, v = jnp.split(x @ w_in, 3, axis=-1)
    u = c_gate * v
    u_pad = jnp.concatenate([u_prev, u], axis=1)
    t = x.shape[1]
    conv = w_dw[0] * u_pad[:, 0:t]
    for j in range(1, CONV_W):
        conv = conv + w_dw[j] * u_pad[:, j:j + t]
    return (b_gate * conv) @ w_out, u_pad[:, -(CONV_W - 1):]


def shared_latent_kv(h, c_act, pos, w_ada_kv, b_ada_kv, g_kv_in, w_dkv, g_kv_lat):
    shift, scale = jnp.split(c_act @ w_ada_kv + b_ada_kv, 2, axis=-1)
    kv_in = modulate(rms_norm(h, g_kv_in), shift, scale)
    lat = kv_in @ w_dkv
    c_kv = rms_norm(lat[..., :KV_RANK], g_kv_lat)
    k_rope = rope(lat[..., KV_RANK:][:, :, None, :], pos)[:, :, 0, :]
    return c_kv, k_rope


def _scores(q_lat, q_rope, c_kv, k_rope):
    s = jnp.einsum('bthr,bsr->bhts', q_lat, c_kv) + jnp.einsum('bthp,bsp->bhts', q_rope, k_rope)
    return s.astype(jnp.float32) * SM_SCALE


def attend_prompt(q_lat, q_rope, c_kv, k_rope, pos):
    b, t, h, r = q_lat.shape
    nb = t // Q_BLOCK

    def one_block(args):
        ql, qr, qp = args
        s = _scores(ql, qr, c_kv, k_rope)
        s = jnp.where(pos[None, :] <= qp[:, None], s, NEG_INF)
        p = jax.nn.softmax(s, axis=-1).astype(c_kv.dtype)
        return jnp.einsum('bhts,bsr->bthr', p, c_kv)

    qlb = q_lat.reshape(b, nb, Q_BLOCK, h, r).swapaxes(0, 1)
    qrb = q_rope.reshape(b, nb, Q_BLOCK, h, ROPE_DIM).swapaxes(0, 1)
    out = lax.map(one_block, (qlb, qrb, pos.reshape(nb, Q_BLOCK)))
    return out.swapaxes(0, 1).reshape(b, t, h, r)


def attend_paged(q_lat, q_rope, c_new, kr_new, pos, past_lat, past_rope):
    s_past = _scores(q_lat, q_rope, past_lat, past_rope)
    s_new = _scores(q_lat, q_rope, c_new, kr_new)
    s_new = jnp.where(pos[None, :] <= pos[:, None], s_new, NEG_INF)
    p = jax.nn.softmax(jnp.concatenate([s_past, s_new], axis=-1), axis=-1).astype(c_new.dtype)
    n_past = past_lat.shape[1]
    return (jnp.einsum('bhts,bsr->bthr', p[..., :n_past], past_lat)
            + jnp.einsum('bhts,bsr->bthr', p[..., n_past:], c_new))


def mla_mixer(a, c_kv, k_rope, pos, attend, w_dq, g_q, w_uq, w_o, w_uk, w_uv):
    b, t, _ = a.shape
    q = (rms_norm(a @ w_dq, g_q) @ w_uq).reshape(b, t, N_HEADS, NOPE_DIM + ROPE_DIM)
    q_nope = q[..., :NOPE_DIM]
    q_rope = rope(q[..., NOPE_DIM:], pos)
    q_lat = jnp.einsum('bthn,rhn->bthr', q_nope, w_uk)
    o_lat = attend(q_lat, q_rope, c_kv, k_rope, pos)
    o = jnp.einsum('bthr,rhv->bthv', o_lat, w_uv).reshape(b, t, N_HEADS * V_DIM)
    return o @ w_o


def moe_ffn(x2d, w_router, b_router, w_gate_up, b_gate_up, w_down, b_down):
    n = x2d.shape[0]
    n_assign = n * TOP_K
    logits = (x2d @ w_router + b_router).astype(jnp.float32)
    top_logit, top_e = lax.top_k(logits, TOP_K)
    gates = jax.nn.softmax(top_logit, axis=-1).reshape(-1)
    flat_e = top_e.reshape(-1)
    order = jnp.argsort(flat_e)
    sorted_e = flat_e[order]
    counts = jnp.bincount(flat_e, length=N_EXPERTS)
    padded = (counts + MOE_BLOCK - 1) // MOE_BLOCK * MOE_BLOCK
    pad_end = jnp.cumsum(padded)
    pad_start = pad_end - padded
    grp_start = jnp.cumsum(counts) - counts
    dest = pad_start[sorted_e] + jnp.arange(n_assign) - grp_start[sorted_e]
    n_rows = (n_assign + N_EXPERTS * (MOE_BLOCK - 1) + MOE_BLOCK - 1) // MOE_BLOCK * MOE_BLOCK
    n_blocks = n_rows // MOE_BLOCK
    slot = jnp.full((n_rows,), n_assign, jnp.int32).at[dest].set(order.astype(jnp.int32))
    valid = slot < n_assign
    slot_c = jnp.minimum(slot, n_assign - 1)
    slot_tok = slot_c // TOP_K
    slot_gate = jnp.where(valid, gates[slot_c], 0.0).astype(x2d.dtype)
    block_e = jnp.minimum(jnp.searchsorted(pad_end, jnp.arange(n_blocks) * MOE_BLOCK, side='right'), N_EXPERTS - 1)

    def expert_block(args):
        tok, e = args
        hgu = x2d[tok] @ w_gate_up[e] + b_gate_up[e]
        gate, up = jnp.split(hgu, 2, axis=-1)
        gate = jnp.minimum(gate, SWIGLU_LIMIT)
        up = jnp.clip(up, -SWIGLU_LIMIT, SWIGLU_LIMIT)
        glu = gate * jax.nn.sigmoid(gate * SWIGLU_ALPHA)
        return ((up + 1) * glu) @ w_down[e] + b_down[e]

    ys = lax.map(expert_block, (slot_tok.reshape(n_blocks, MOE_BLOCK), block_e))
    return jnp.zeros_like(x2d).at[slot_tok].add(ys.reshape(n_rows, -1) * slot_gate[:, None])


def trunk(x, c, pos, conv_prev, attend, p):
    h = x
    c_act = jax.nn.silu(c)
    conv_states = []
    c_kv = None
    k_rope = None
    for l in range(DEPTH):
        mod = c_act @ p['w_ada'][l] + p['b_ada'][l]
        sh_m, sc_m, gt_m, sh_f, sc_f, gt_f = jnp.split(mod, 6, axis=-1)
        a = modulate(rms_norm(h, p['g_pre_mix'][l]), sh_m, sc_m)
        if l < N_A_LAYERS:
            mix, st = short_conv_mixer(a, conv_prev[l], p['w_conv_in'][l], p['w_conv_dw'][l], p['w_conv_out'][l])
            conv_states.append(st)
        else:
            if l == N_A_LAYERS:
                c_kv, k_rope = shared_latent_kv(h, c_act, pos, p['w_ada_kv'], p['b_ada_kv'], p['g_kv_in'],
                                                p['w_dkv'], p['g_kv_lat'])
            j = l - N_A_LAYERS
            mix = mla_mixer(a, c_kv, k_rope, pos, attend, p['w_dq'][j], p['g_q'][j], p['w_uq'][j],
                            p['w_o'][j], p['w_uk'], p['w_uv'])
        h = h + gt_m[:, None, :] * rms_norm(mix, p['g_post_mix'][l])
        f = modulate(rms_norm(h, p['g_pre_ffn'][l]), sh_f, sc_f)
        f = moe_ffn(f.reshape(-1, f.shape[-1]), p['w_router'][l], p['b_router'][l], p['w_gate_up'][l],
                    p['b_gate_up'][l], p['w_down'][l], p['b_down'][l]).reshape(h.shape)
        h = h + gt_f[:, None, :] * rms_norm(f, p['g_post_ffn'][l])
    return h, jnp.stack(conv_states), c_kv, k_rope


def setup_inputs(seed: int = 0) -> dict:
    key = jax.random.key(seed)
    ks = iter(list(jax.random.split(key, 40)))

    def nrm(shape, scale=1.0):
        return jax.random.normal(next(ks), shape, jnp.float32) * scale

    def gain(shape):
        return 1.0 + nrm(shape, 0.02)

    n_pages = PAST_LEN // PAGE_SIZE
    n_used = DEC_BATCH * n_pages
    n_pool = n_used + n_used // 4
    page_table = jax.random.permutation(next(ks), n_pool)[:n_used].reshape(DEC_BATCH, n_pages).astype(jnp.int32)
    d = D_MODEL
    return {
        'x_prompt': nrm((BATCH, SEQ, d)),
        'x_sample': nrm((DEC_BATCH, DEC_SEQ, d)),
        'cache_kv_latent': nrm((n_pool, PAGE_SIZE, KV_RANK)),
        'cache_k_rope': nrm((n_pool, PAGE_SIZE, ROPE_DIM)),
        'state_conv': nrm((N_A_LAYERS, DEC_BATCH, CONV_W - 1, d)),
        'page_table': page_table,
        'c_prompt': nrm((BATCH, d)),
        'c_sample': nrm((DEC_BATCH, d)),
        'w_ada': nrm((DEPTH, d, 6 * d), 0.5 * d ** -0.5),
        'b_ada': nrm((DEPTH, 6 * d), 0.02),
        'g_pre_mix': gain((DEPTH, d)),
        'g_post_mix': gain((DEPTH, d)),
        'g_pre_ffn': gain((DEPTH, d)),
        'g_post_ffn': gain((DEPTH, d)),
        'w_conv_in': nrm((N_A_LAYERS, d, 3 * d), d ** -0.5),
        'w_conv_dw': nrm((N_A_LAYERS, CONV_W, d), CONV_W ** -0.5),
        'w_conv_out': nrm((N_A_LAYERS, d, d), d ** -0.5),
        'w_ada_kv': nrm((d, 2 * d), 0.5 * d ** -0.5),
        'b_ada_kv': nrm((2 * d,), 0.02),
        'g_kv_in': gain((d,)),
        'w_dkv': nrm((d, KV_RANK + ROPE_DIM), d ** -0.5),
        'g_kv_lat': gain((KV_RANK,)),
        'w_uk': nrm((KV_RANK, N_HEADS, NOPE_DIM), KV_RANK ** -0.5),
        'w_uv': nrm((KV_RANK, N_HEADS, V_DIM), KV_RANK ** -0.5),
        'w_dq': nrm((N_B_LAYERS, d, Q_RANK), d ** -0.5),
        'g_q': gain((N_B_LAYERS, Q_RANK)),
        'w_uq': nrm((N_B_LAYERS, Q_RANK, N_HEADS * (NOPE_DIM + ROPE_DIM)), Q_RANK ** -0.5),
        'w_o': nrm((N_B_LAYERS, N_HEADS * V_DIM, d), (N_HEADS * V_DIM) ** -0.5),
        'w_router': nrm((DEPTH, d, N_EXPERTS), d ** -0.5),
        'b_router': nrm((DEPTH, N_EXPERTS), 0.01),
        'w_gate_up': nrm((DEPTH, N_EXPERTS, d, 2 * D_FF), d ** -0.5),
        'b_gate_up': nrm((DEPTH, N_EXPERTS, 2 * D_FF), 0.02),
        'w_down': nrm((DEPTH, N_EXPERTS, D_FF, d), D_FF ** -0.5),
        'b_down': nrm((DEPTH, N_EXPERTS, d), 0.02),
    }


def reference(x_prompt, x_sample, cache_kv_latent, cache_k_rope, state_conv, page_table, c_prompt, c_sample,
              w_ada, b_ada, g_pre_mix, g_post_mix, g_pre_ffn, g_post_ffn, w_conv_in, w_conv_dw, w_conv_out,
              w_ada_kv, b_ada_kv, g_kv_in, w_dkv, g_kv_lat, w_uk, w_uv, w_dq, g_q, w_uq, w_o,
              w_router, b_router, w_gate_up, b_gate_up, w_down, b_down):
    p = dict(w_ada=w_ada, b_ada=b_ada, g_pre_mix=g_pre_mix, g_post_mix=g_post_mix, g_pre_ffn=g_pre_ffn,
             g_post_ffn=g_post_ffn, w_conv_in=w_conv_in, w_conv_dw=w_conv_dw, w_conv_out=w_conv_out,
             w_ada_kv=w_ada_kv, b_ada_kv=b_ada_kv, g_kv_in=g_kv_in, w_dkv=w_dkv, g_kv_lat=g_kv_lat,
             w_uk=w_uk, w_uv=w_uv, w_dq=w_dq, g_q=g_q, w_uq=w_uq, w_o=w_o, w_router=w_router,
             b_router=b_router, w_gate_up=w_gate_up, b_gate_up=b_gate_up, w_down=w_down, b_down=b_down)

    pos_prompt = jnp.arange(x_prompt.shape[1], dtype=jnp.int32)
    conv0 = jnp.zeros((N_A_LAYERS, x_prompt.shape[0], CONV_W - 1, x_prompt.shape[2]), x_prompt.dtype)
    y_prompt, conv_prompt, lat_prompt, rope_prompt = trunk(x_prompt, c_prompt, pos_prompt, conv0,
                                                           attend_prompt, p)

    n_seq = page_table.shape[0]
    n_past = page_table.shape[1] * PAGE_SIZE
    past_lat = jnp.take(cache_kv_latent, page_table, axis=0).reshape(n_seq, n_past, KV_RANK)
    past_rope = jnp.take(cache_k_rope, page_table, axis=0).reshape(n_seq, n_past, ROPE_DIM)

    def attend_sample(q_lat, q_rope, c_new, kr_new, pos):
        return attend_paged(q_lat, q_rope, c_new, kr_new, pos, past_lat, past_rope)

    pos_sample = n_past + jnp.arange(x_sample.shape[1], dtype=jnp.int32)
    y_sample, conv_sample, lat_sample, rope_sample = trunk(x_sample, c_sample, pos_sample, state_conv,
                                                           attend_sample, p)
    return (y_prompt, y_sample, lat_prompt, rope_prompt, conv_prompt, lat_sample, rope_sample, conv_sample)
```

```python
import functools
import math

import jax
import jax.numpy as jnp
import numpy as np
from jax import lax
from jax.experimental import pallas as pl
from jax.experimental.pallas import tpu as pltpu

N_HEADS = 8
NOPE_DIM = 128
ROPE_DIM = 64
HALF_ROPE = ROPE_DIM // 2
V_DIM = 128
KV_RANK = 256
ROPE_THETA = 10000.0
SM_SCALE = (NOPE_DIM + ROPE_DIM) ** -0.5
N_EXPERTS = 32
TOP_K = 4
SWIGLU_LIMIT = 7.0
SWIGLU_ALPHA = 1.702
RMS_EPS = 1e-6
NEG_INF = -1e30
PAGE_SIZE = 128
CONV_W = 3
TAIL_ROWS = 8

VMEM_LIMIT = 56 * 1024 * 1024
BF16 = jnp.bfloat16
F32 = jnp.float32


def _cparams(sem):
    return pltpu.CompilerParams(dimension_semantics=sem, vmem_limit_bytes=VMEM_LIMIT)


def _const_spec(shape):
    nd = len(shape)
    return pl.BlockSpec(shape, lambda *_: (0,) * nd)


def _rms(x, g):
    ms = jnp.mean(x * x, axis=-1, keepdims=True)
    return x * lax.rsqrt(ms + RMS_EPS) * g


def _dot(a, b):
    return jnp.dot(a.astype(BF16), b.astype(BF16), preferred_element_type=F32)


def _dot_nt(a, b):
    return lax.dot_general(a.astype(BF16), b.astype(BF16), (((1,), (1,)), ((), ())),
                           preferred_element_type=F32)


def _modvec(mod_ref, j, bb, tt):
    v = mod_ref[:, j:j + 1, :]
    d = v.shape[-1]
    if bb == 1:
        return v[0]
    return jnp.broadcast_to(v, (bb, tt, d)).reshape(bb * tt, d)


def _ada_kernel(c_ref, w_ref, b_ref, o_ref):
    c = c_ref[...]
    c_act = c * jax.nn.sigmoid(c)
    o_ref[...] = _dot(c_act, w_ref[...]) + b_ref[...]


def _ada(c, w, b, tn=1024):
    m, d = c.shape
    n = w.shape[1]
    return pl.pallas_call(
        _ada_kernel,
        out_shape=jax.ShapeDtypeStruct((m, n), F32),
        grid=(n // tn,),
        in_specs=[_const_spec((m, d)), pl.BlockSpec((d, tn), lambda j: (0, j)),
                  pl.BlockSpec((1, tn), lambda j: (0, j))],
        out_specs=pl.BlockSpec((m, tn), lambda j: (0, j)),
        compiler_params=_cparams(("arbitrary",)),
        name="ada",
    )(c, w, b.reshape(1, n))


def _stage_tail(first_step, h, mix, gt_m, sh_f, sc_f, g_post, g_pre_ffn, w_router, b_router,
                cnt_ref, h_out, f_out, e_out, g_out, r_out, c_out, bb, tt):
    rows = bb * tt
    h1 = h + gt_m * _rms(mix, g_post)
    f = _rms(h1, g_pre_ffn) * (1.0 + sc_f) + sh_f
    h_out[...] = h1.reshape(bb, tt, -1)
    f_out[...] = f.reshape(bb, tt, -1)

    logits = _dot(f, w_router) + b_router
    lane = lax.broadcasted_iota(jnp.int32, logits.shape, 1)
    lane_k = lax.broadcasted_iota(jnp.int32, (rows, TOP_K), 1)
    work = logits
    multi_hot = jnp.zeros(logits.shape, F32)
    top_val, top_idx, top_sel = [], [], []
    for _ in range(TOP_K):
        m = jnp.max(work, axis=-1, keepdims=True)
        idx = jnp.min(jnp.where(work == m, lane, N_EXPERTS), axis=-1, keepdims=True)
        sel = lane == idx
        multi_hot = multi_hot + sel.astype(F32)
        work = jnp.where(sel, -jnp.inf, work)
        top_val.append(m)
        top_idx.append(idx)
        top_sel.append(sel)
    exps = [jnp.exp(v - top_val[0]) for v in top_val]
    denom = exps[0] + exps[1] + exps[2] + exps[3]

    @pl.when(first_step)
    def _():
        cnt_ref[...] = jnp.zeros_like(cnt_ref)

    ri = lax.broadcasted_iota(jnp.int32, (rows, rows), 0)
    ci = lax.broadcasted_iota(jnp.int32, (rows, rows), 1)
    strict_lower = (ri > ci).astype(BF16)
    before = jnp.dot(strict_lower, multi_hot.astype(BF16), preferred_element_type=F32) + cnt_ref[...]

    e_val = jnp.zeros((rows, TOP_K), jnp.int32)
    g_val = jnp.zeros((rows, TOP_K), F32)
    r_val = jnp.zeros((rows, TOP_K), F32)
    for k in range(TOP_K):
        rank_k = jnp.sum(jnp.where(top_sel[k], before, 0.0), axis=-1, keepdims=True)
        e_val = jnp.where(lane_k == k, top_idx[k], e_val)
        g_val = jnp.where(lane_k == k, exps[k] / denom, g_val)
        r_val = jnp.where(lane_k == k, rank_k, r_val)
    e_out[...] = e_val.reshape(bb, tt, TOP_K)
    g_out[...] = g_val.reshape(bb, tt, TOP_K)
    r_out[...] = r_val.astype(jnp.int32).reshape(bb, tt, TOP_K)
    cnt_ref[...] = cnt_ref[...] + jnp.sum(multi_hot, axis=0, keepdims=True)
    c_out[...] = cnt_ref[...]


def _tail_out_shapes(b, t, d):
    return [jax.ShapeDtypeStruct((b, t, d), F32), jax.ShapeDtypeStruct((b, t, d), F32),
            jax.ShapeDtypeStruct((b, t, TOP_K), jnp.int32), jax.ShapeDtypeStruct((b, t, TOP_K), F32),
            jax.ShapeDtypeStruct((b, t, TOP_K), jnp.int32), jax.ShapeDtypeStruct((1, N_EXPERTS), F32)]


def _tail_out_specs(bb, tt, d):
    tok = lambda n: pl.BlockSpec((bb, tt, n), lambda i, j: (i, j, 0))
    return [tok(d), tok(d), tok(TOP_K), tok(TOP_K), tok(TOP_K),
            pl.BlockSpec((1, N_EXPERTS), lambda i, j: (0, 0))]


def _conv_stage_kernel(x_ref, mod_ref, prev_ref, g_pre, w_in, w_dw, w_out, g_post, g_pre_ffn,
                       w_router, b_router,
                       h_out, f_out, e_out, g_out, r_out, c_out, u_tail,
                       carry, cnt_ref, *, bb, tt, d):
    rows = bb * tt
    i, j = pl.program_id(0), pl.program_id(1)
    x = x_ref[...].reshape(rows, d)
    mv = lambda k: _modvec(mod_ref, k, bb, tt)
    a = _rms(x, g_pre[...]) * (1.0 + mv(1)) + mv(0)
    bcv = _dot(a, w_in[...])
    b_gate, c_gate, v = bcv[:, :d], bcv[:, d:2 * d], bcv[:, 2 * d:]
    u = c_gate * v

    tpos = lax.broadcasted_iota(jnp.int32, (rows, 1), 0) % tt
    if bb == 1:
        @pl.when(j == 0)
        def _():
            carry[...] = prev_ref[0]
        p0, p1 = carry[TAIL_ROWS - 2:TAIL_ROWS - 1, :], carry[TAIL_ROWS - 1:TAIL_ROWS, :]
        u_m1 = jnp.where(tpos == 0, p1, pltpu.roll(u, 1, 0))
        u_m2 = jnp.where(tpos == 0, p0, jnp.where(tpos == 1, p1, pltpu.roll(u, 2, 0)))
        carry[...] = u[rows - TAIL_ROWS:, :]
        u_tail[0] = u[rows - TAIL_ROWS:, :]
    else:
        pp = prev_ref[...].reshape(rows, d)
        u_m1 = jnp.where(tpos == 0, pltpu.roll(pp, rows - (TAIL_ROWS - 1), 0), pltpu.roll(u, 1, 0))
        u_m2 = jnp.where(tpos < 2, pltpu.roll(pp, rows - (TAIL_ROWS - 2), 0), pltpu.roll(u, 2, 0))
        u_tail[...] = u.reshape(bb, tt, d)
    conv = w_dw[0:1, :] * u_m2 + w_dw[1:2, :] * u_m1 + w_dw[2:3, :] * u
    mix = _dot(b_gate * conv, w_out[...])

    _stage_tail(jnp.logical_and(i == 0, j == 0), x, mix, mv(2), mv(3), mv(4), g_post[...], g_pre_ffn[...],
                w_router[...], b_router[...], cnt_ref, h_out, f_out, e_out, g_out, r_out, c_out, bb, tt)


def _conv_stage(x, mod, prev_pad, g_pre, w_in, w_dw, w_out, g_post, g_pre_ffn, w_router, b_router, bb, tt):
    b, t, d = x.shape
    assert b % bb == 0 and t % tt == 0 and (bb == 1 or tt == t == TAIL_ROWS)
    tok = lambda n: pl.BlockSpec((bb, tt, n), lambda i, j: (i, j, 0))
    seq = lambda n: pl.BlockSpec((bb, n, d), lambda i, j: (i, 0, 0))
    kern = functools.partial(_conv_stage_kernel, bb=bb, tt=tt, d=d)
    outs = pl.pallas_call(
        kern,
        out_shape=_tail_out_shapes(b, t, d) + [jax.ShapeDtypeStruct((b, TAIL_ROWS, d), F32)],
        grid=(b // bb, t // tt),
        in_specs=[tok(d), seq(6), seq(TAIL_ROWS), _const_spec((1, d)), _const_spec((d, 3 * d)),
                  _const_spec((CONV_W, d)), _const_spec((d, d)), _const_spec((1, d)), _const_spec((1, d)),
                  _const_spec((d, N_EXPERTS)), _const_spec((1, N_EXPERTS))],
        out_specs=_tail_out_specs(bb, tt, d) + [seq(TAIL_ROWS)],
        scratch_shapes=[pltpu.VMEM((TAIL_ROWS, d), F32), pltpu.VMEM((1, N_EXPERTS), F32)],
        compiler_params=_cparams(("arbitrary", "arbitrary")),
        name="conv_stage",
    )(x, mod, prev_pad, g_pre, w_in, w_dw, w_out, g_post, g_pre_ffn, w_router, b_router)
    return outs


def _moe_kernel(tok_ref, be_ref, nb_ref, x_hbm, wgu, bgu, wd, bd, o_ref, xbuf, sem, *, bm, d_ff):
    i = pl.program_id(0)
    nb = nb_ref[0]
    slot = i % 2

    def row_copy(blk, s, r):
        return pltpu.make_async_copy(x_hbm.at[pl.ds(tok_ref[blk, r], 1)], xbuf.at[s, pl.ds(r, 1)], sem.at[s])

    def issue(blk, s):
        def body(r, c):
            row_copy(blk, s, r).start()
            return c
        lax.fori_loop(0, bm, body, 0)

    @pl.when(i == 0)
    def _():
        issue(0, 0)

    @pl.when(i + 1 < nb)
    def _():
        issue(i + 1, 1 - slot)

    @pl.when(i < nb)
    def _():
        pltpu.make_async_copy(x_hbm.at[pl.ds(0, bm)], xbuf.at[slot], sem.at[slot]).wait()
        x = xbuf[slot]
        hgu = _dot(x, wgu[0]) + bgu[0]
        gate = jnp.minimum(hgu[:, :d_ff], SWIGLU_LIMIT)
        up = jnp.clip(hgu[:, d_ff:], -SWIGLU_LIMIT, SWIGLU_LIMIT)
        glu = gate * jax.nn.sigmoid(gate * SWIGLU_ALPHA)
        o_ref[...] = _dot((up + 1.0) * glu, wd[0]) + bd[0]

    @pl.when(i >= nb)
    def _():
        o_ref[...] = jnp.zeros_like(o_ref)


def _moe(x2d, slot_tok, block_e, nb_used, wgu, bgu, wd, bd, bm):
    n, d = x2d.shape
    n_blocks = slot_tok.shape[0]
    d_ff = wd.shape[1]
    kern = functools.partial(_moe_kernel, bm=bm, d_ff=d_ff)
    grid_spec = pltpu.PrefetchScalarGridSpec(
        num_scalar_prefetch=3,
        grid=(n_blocks,),
        in_specs=[pl.BlockSpec(memory_space=pl.ANY),
                  pl.BlockSpec((1, d, 2 * d_ff), lambda i, t, e, nb: (e[i], 0, 0)),
                  pl.BlockSpec((1, 1, 2 * d_ff), lambda i, t, e, nb: (e[i], 0, 0)),
                  pl.BlockSpec((1, d_ff, d), lambda i, t, e, nb: (e[i], 0, 0)),
                  pl.BlockSpec((1, 1, d), lambda i, t, e, nb: (e[i], 0, 0))],
        out_specs=pl.BlockSpec((bm, d), lambda i, t, e, nb: (i, 0)),
        scratch_shapes=[pltpu.VMEM((2, bm, d), F32), pltpu.SemaphoreType.DMA((2,))],
    )
    return pl.pallas_call(
        kern,
        out_shape=jax.ShapeDtypeStruct((n_blocks * bm, d), F32),
        grid_spec=grid_spec,
        compiler_params=_cparams(("arbitrary",)),
        name="moe",
    )(slot_tok, block_e, nb_used, x2d, wgu, bgu, wd, bd)


def _combine_kernel(dest_ref, y_hbm, gates_ref, h_ref, mod_ref, g_post, o_ref, ybuf, sem, *, bb, tt, d, n_j):
    rows = bb * tt
    i, j = pl.program_id(0), pl.program_id(1)
    step = i * n_j + j
    n_steps = pl.num_programs(0) * n_j
    slot = step % 2

    def issue(st, s):
        def body(r, c):
            for k in range(TOP_K):
                pltpu.make_async_copy(y_hbm.at[pl.ds(dest_ref[st, r * TOP_K + k], 1)],
                                      ybuf.at[s, k, pl.ds(r, 1)], sem.at[s]).start()
            return c
        lax.fori_loop(0, rows, body, 0)

    @pl.when(step == 0)
    def _():
        issue(0, 0)

    @pl.when(step + 1 < n_steps)
    def _():
        issue(step + 1, 1 - slot)

    for k in range(TOP_K):
        pltpu.make_async_copy(y_hbm.at[pl.ds(0, rows)], ybuf.at[slot, k], sem.at[slot]).wait()
    g = gates_ref[...].reshape(rows, TOP_K)
    f = g[:, 0:1] * ybuf[slot, 0]
    for k in range(1, TOP_K):
        f = f + g[:, k:k + 1] * ybuf[slot, k]
    h = h_ref[...].reshape(rows, d)
    gt_f = _modvec(mod_ref, 5, bb, tt)
    o_ref[...] = (h + gt_f * _rms(f, g_post[...])).reshape(bb, tt, d)


def _combine(ys, dest, gates, h1, mod, g_post, bb, tt):
    b, t, d = h1.shape
    n_j = t // tt
    rows = bb * tt
    n_steps = (b // bb) * n_j
    dest2d = dest.reshape(n_steps, rows * TOP_K)
    tok = lambda n: pl.BlockSpec((bb, tt, n), lambda i, j, dr: (i, j, 0))
    kern = functools.partial(_combine_kernel, bb=bb, tt=tt, d=d, n_j=n_j)
    grid_spec = pltpu.PrefetchScalarGridSpec(
        num_scalar_prefetch=1,
        grid=(b // bb, n_j),
        in_specs=[pl.BlockSpec(memory_space=pl.ANY), tok(TOP_K), tok(d),
                  pl.BlockSpec((bb, 6, d), lambda i, j, dr: (i, 0, 0)),
                  pl.BlockSpec((1, d), lambda i, j, dr: (0, 0))],
        out_specs=tok(d),
        scratch_shapes=[pltpu.VMEM((2, TOP_K, rows, d), F32), pltpu.SemaphoreType.DMA((2,))],
    )
    return pl.pallas_call(
        kern,
        out_shape=jax.ShapeDtypeStruct((b, t, d), F32),
        grid_spec=grid_spec,
        compiler_params=_cparams(("arbitrary", "arbitrary")),
        name="combine",
    )(dest2d, ys, gates, h1, mod, g_post)


def _mla_pre_kernel(h_ref, mod_ref, modkv_ref, g_pre, g_kv_in, w_dkv, g_kv_lat, ck_ref, sk_ref,
                    w_dq, g_q, w_uq, w_ukt, cq_ref, sq_ref,
                    ckv_out, kr_out, ql_out, qr_out, *, bb, tt, d):
    rows = bb * tt
    h = h_ref[...].reshape(rows, d)
    ms = jnp.mean(h * h, axis=-1, keepdims=True)
    hn = h * lax.rsqrt(ms + RMS_EPS)

    kv_in = (hn * g_kv_in[...]) * (1.0 + _modvec(modkv_ref, 1, bb, tt)) + _modvec(modkv_ref, 0, bb, tt)
    lat = _dot(kv_in, w_dkv[...])
    ckv_out[...] = _rms(lat[:, :KV_RANK], g_kv_lat[...]).reshape(bb, tt, KV_RANK)
    kx = lat[:, KV_RANK:KV_RANK + ROPE_DIM].reshape(bb, tt, ROPE_DIM)
    kxs = lat[:, KV_RANK + ROPE_DIM:].reshape(bb, tt, ROPE_DIM)
    kr_out[...] = kx * ck_ref[...][None] + kxs * sk_ref[...][None]

    a = (hn * g_pre[...]) * (1.0 + _modvec(mod_ref, 1, bb, tt)) + _modvec(mod_ref, 0, bb, tt)
    qc = _rms(_dot(a, w_dq[...]), g_q[...])
    q = _dot(qc, w_uq[...])
    n_nope, n_rope = N_HEADS * NOPE_DIM, N_HEADS * ROPE_DIM
    for hd in range(N_HEADS):
        q_lat = _dot(q[:, hd * NOPE_DIM:(hd + 1) * NOPE_DIM], w_ukt[hd])
        ql_out[:, :, hd * KV_RANK:(hd + 1) * KV_RANK] = q_lat.astype(BF16).reshape(bb, tt, KV_RANK)
    qx = q[:, n_nope:n_nope + n_rope].reshape(bb, tt, n_rope)
    qxs = q[:, n_nope + n_rope:].reshape(bb, tt, n_rope)
    qr_out[...] = (qx * cq_ref[...][None] + qxs * sq_ref[...][None]).astype(BF16)


def _mla_pre(h, mod, modkv, g_pre, g_kv_in, w_dkv, g_kv_lat, ck, sk, w_dq, g_q, w_uq, w_ukt, cq, sq, bb, tt):
    b, t, d = h.shape
    q_rank = w_dq.shape[1]
    tok = lambda n: pl.BlockSpec((bb, tt, n), lambda i, j: (i, j, 0))
    seq = lambda n: pl.BlockSpec((bb, n, d), lambda i, j: (i, 0, 0))
    pos = lambda n: pl.BlockSpec((tt, n), lambda i, j: (j, 0))
    kern = functools.partial(_mla_pre_kernel, bb=bb, tt=tt, d=d)
    return pl.pallas_call(
        kern,
        out_shape=[jax.ShapeDtypeStruct((b, t, KV_RANK), F32), jax.ShapeDtypeStruct((b, t, ROPE_DIM), F32),
                   jax.ShapeDtypeStruct((b, t, N_HEADS * KV_RANK), BF16),
                   jax.ShapeDtypeStruct((b, t, N_HEADS * ROPE_DIM), BF16)],
        grid=(b // bb, t // tt),
        in_specs=[tok(d), seq(6), seq(2), _const_spec((1, d)), _const_spec((1, d)),
                  _const_spec(w_dkv.shape), _const_spec((1, KV_RANK)), pos(ROPE_DIM), pos(ROPE_DIM),
                  _const_spec(w_dq.shape), _const_spec((1, q_rank)), _const_spec(w_uq.shape),
                  _const_spec(w_ukt.shape), pos(N_HEADS * ROPE_DIM), pos(N_HEADS * ROPE_DIM)],
        out_specs=[tok(KV_RANK), tok(ROPE_DIM), tok(N_HEADS * KV_RANK), tok(N_HEADS * ROPE_DIM)],
        compiler_params=_cparams(("arbitrary", "arbitrary")),
        name="mla_pre",
    )(h, mod, modkv, g_pre, g_kv_in, w_dkv, g_kv_lat, ck, sk, w_dq, g_q, w_uq, w_ukt, cq, sq)


def _online_update(s, v_bf16, m_ref, l_ref, acc_ref, idx):
    m_prev = m_ref[idx]
    m_new = jnp.maximum(m_prev, jnp.max(s, axis=-1, keepdims=True))
    alpha = jnp.exp(m_prev - m_new)
    p = jnp.exp(s - m_new)
    l_ref[idx] = alpha * l_ref[idx] + jnp.sum(p, axis=-1, keepdims=True)
    acc_ref[idx] = alpha * acc_ref[idx] + jnp.dot(p.astype(BF16), v_bf16, preferred_element_type=F32)
    m_ref[idx] = m_new


def _attn_prompt_kernel(ql_ref, qr_ref, ckv_ref, kr_ref, o_ref, m_ref, l_ref, acc_ref, *, tq):
    i, j = pl.program_id(1), pl.program_id(2)

    @pl.when(j == 0)
    def _():
        m_ref[...] = jnp.full_like(m_ref, -jnp.inf)
        l_ref[...] = jnp.zeros_like(l_ref)
        acc_ref[...] = jnp.zeros_like(acc_ref)

    def step(masked):
        ckv = ckv_ref[0].astype(BF16)
        kr = kr_ref[0].astype(BF16)
        if masked:
            qi = lax.broadcasted_iota(jnp.int32, (tq, tq), 0)
            ki = lax.broadcasted_iota(jnp.int32, (tq, tq), 1)
            visible = ki <= qi
        for hd in range(N_HEADS):
            s = _dot_nt(ql_ref[0, :, hd * KV_RANK:(hd + 1) * KV_RANK], ckv)
            s = s + _dot_nt(qr_ref[0, :, hd * ROPE_DIM:(hd + 1) * ROPE_DIM], kr)
            s = s * SM_SCALE
            if masked:
                s = jnp.where(visible, s, NEG_INF)
            _online_update(s, ckv, m_ref, l_ref, acc_ref, hd)

    @pl.when(j < i)
    def _():
        step(False)

    @pl.when(j == i)
    def _():
        step(True)
        for hd in range(N_HEADS):
            o_ref[0, :, hd * KV_RANK:(hd + 1) * KV_RANK] = (acc_ref[hd] / l_ref[hd]).astype(BF16)


def _attn_prompt(q_lat, q_rope, c_kv, k_rope, tq):
    b, t, _ = q_lat.shape
    n = t // tq
    qspec = lambda w: pl.BlockSpec((1, tq, w), lambda bi, i, j: (bi, i, 0))
    kspec = lambda w: pl.BlockSpec((1, tq, w), lambda bi, i, j: (bi, jnp.minimum(i, j), 0))
    return pl.pallas_call(
        functools.partial(_attn_prompt_kernel, tq=tq),
        out_shape=jax.ShapeDtypeStruct((b, t, N_HEADS * KV_RANK), BF16),
        grid=(b, n, n),
        in_specs=[qspec(N_HEADS * KV_RANK), qspec(N_HEADS * ROPE_DIM), kspec(KV_RANK), kspec(ROPE_DIM)],
        out_specs=qspec(N_HEADS * KV_RANK),
        scratch_shapes=[pltpu.VMEM((N_HEADS, tq, 1), F32), pltpu.VMEM((N_HEADS, tq, 1), F32),
                        pltpu.VMEM((N_HEADS, tq, KV_RANK), F32)],
        compiler_params=_cparams(("arbitrary", "arbitrary", "arbitrary")),
        name="attn_prompt",
    )(q_lat, q_rope, c_kv, k_rope)


def _attn_paged_kernel(pt_ref, ql_ref, qr_ref, cnew_ref, krnew_ref, *rest, ppc, t_new):
    lat_pages = rest[:ppc]
    rope_pages = rest[ppc:2 * ppc]
    o_ref = rest[2 * ppc]
    qs, qrs, kbuf, krbuf, m_ref, l_ref, acc_ref = rest[2 * ppc + 1:]
    j = pl.program_id(1)
    rows = N_HEADS * t_new

    @pl.when(j == 0)
    def _():
        for hd in range(N_HEADS):
            qs[hd * t_new:(hd + 1) * t_new, :] = ql_ref[0, :, hd * KV_RANK:(hd + 1) * KV_RANK].astype(F32)
            qrs[hd * t_new:(hd + 1) * t_new, :] = qr_ref[0, :, hd * ROPE_DIM:(hd + 1) * ROPE_DIM].astype(F32)
        m_ref[...] = jnp.full_like(m_ref, -jnp.inf)
        l_ref[...] = jnp.zeros_like(l_ref)
        acc_ref[...] = jnp.zeros_like(acc_ref)

    for k in range(ppc):
        kbuf[k * PAGE_SIZE:(k + 1) * PAGE_SIZE, :] = lat_pages[k][0].astype(BF16)
        krbuf[k * PAGE_SIZE:(k + 1) * PAGE_SIZE, :] = rope_pages[k][0].astype(BF16)
    q, qr = qs[...], qrs[...]
    kv = kbuf[...]
    s = (_dot_nt(q, kv) + _dot_nt(qr, krbuf[...])) * SM_SCALE
    _online_update(s, kv, m_ref, l_ref, acc_ref, 0)

    @pl.when(j == pl.num_programs(1) - 1)
    def _():
        c_new = cnew_ref[0].astype(BF16)
        s_new = (_dot_nt(q, c_new) + _dot_nt(qr, krnew_ref[0])) * SM_SCALE
        q_pos = lax.broadcasted_iota(jnp.int32, (rows, t_new), 0) % t_new
        k_pos = lax.broadcasted_iota(jnp.int32, (rows, t_new), 1)
        s_new = jnp.where(k_pos <= q_pos, s_new, NEG_INF)
        _online_update(s_new, c_new, m_ref, l_ref, acc_ref, 0)
        out = acc_ref[0] / l_ref[0]
        for hd in range(N_HEADS):
            o_ref[0, :, hd * KV_RANK:(hd + 1) * KV_RANK] = out[hd * t_new:(hd + 1) * t_new, :].astype(BF16)


def _attn_paged(q_lat, q_rope, c_new, kr_new, cache_lat, cache_rope, page_table, ppc):
    s, t_new, _ = q_lat.shape
    n_pages = page_table.shape[1]
    assert n_pages % ppc == 0
    rows = N_HEADS * t_new
    seq = lambda w: pl.BlockSpec((1, t_new, w), lambda b, j, pt: (b, 0, 0))
    page = lambda w, k: pl.BlockSpec((1, PAGE_SIZE, w), lambda b, j, pt: (pt[b, j * ppc + k], 0, 0))
    grid_spec = pltpu.PrefetchScalarGridSpec(
        num_scalar_prefetch=1,
        grid=(s, n_pages // ppc),
        in_specs=[seq(N_HEADS * KV_RANK), seq(N_HEADS * ROPE_DIM), seq(KV_RANK), seq(ROPE_DIM)]
                 + [page(KV_RANK, k) for k in range(ppc)] + [page(ROPE_DIM, k) for k in range(ppc)],
        out_specs=seq(N_HEADS * KV_RANK),
        scratch_shapes=[pltpu.VMEM((rows, KV_RANK), F32), pltpu.VMEM((rows, ROPE_DIM), F32),
                        pltpu.VMEM((ppc * PAGE_SIZE, KV_RANK), BF16), pltpu.VMEM((ppc * PAGE_SIZE, ROPE_DIM), BF16),
                        pltpu.VMEM((1, rows, 1), F32), pltpu.VMEM((1, rows, 1), F32),
                        pltpu.VMEM((1, rows, KV_RANK), F32)],
    )
    return pl.pallas_call(
        functools.partial(_attn_paged_kernel, ppc=ppc, t_new=t_new),
        out_shape=jax.ShapeDtypeStruct((s, t_new, N_HEADS * KV_RANK), BF16),
        grid_spec=grid_spec,
        compiler_params=_cparams(("arbitrary", "arbitrary")),
        name="attn_paged",
    )(page_table, q_lat, q_rope, c_new, kr_new, *([cache_lat] * ppc), *([cache_rope] * ppc))


def _mla_post_kernel(h_ref, ol_ref, mod_ref, w_uv, w_o, g_post, g_pre_ffn, w_router, b_router,
                     h_out, f_out, e_out, g_out, r_out, c_out, obuf, cnt_ref, *, bb, tt, d):
    rows = bb * tt
    i, j = pl.program_id(0), pl.program_id(1)
    h = h_ref[...].reshape(rows, d)
    for hd in range(N_HEADS):
        o_lat = ol_ref[:, :, hd * KV_RANK:(hd + 1) * KV_RANK].reshape(rows, KV_RANK)
        obuf[:, hd * V_DIM:(hd + 1) * V_DIM] = jnp.dot(o_lat, w_uv[hd], preferred_element_type=F32).astype(BF16)
    mix = jnp.dot(obuf[...], w_o[...], preferred_element_type=F32)
    mv = lambda k: _modvec(mod_ref, k, bb, tt)
    _stage_tail(jnp.logical_and(i == 0, j == 0), h, mix, mv(2), mv(3), mv(4), g_post[...], g_pre_ffn[...],
                w_router[...], b_router[...], cnt_ref, h_out, f_out, e_out, g_out, r_out, c_out, bb, tt)


def _mla_post(h, o_lat, mod, w_uv, w_o, g_post, g_pre_ffn, w_router, b_router, bb, tt):
    b, t, d = h.shape
    tok = lambda n: pl.BlockSpec((bb, tt, n), lambda i, j: (i, j, 0))
    kern = functools.partial(_mla_post_kernel, bb=bb, tt=tt, d=d)
    return pl.pallas_call(
        kern,
        out_shape=_tail_out_shapes(b, t, d),
        grid=(b // bb, t // tt),
        in_specs=[tok(d), tok(N_HEADS * KV_RANK), pl.BlockSpec((bb, 6, d), lambda i, j: (i, 0, 0)),
                  _const_spec(w_uv.shape), _const_spec(w_o.shape), _const_spec((1, d)), _const_spec((1, d)),
                  _const_spec((d, N_EXPERTS)), _const_spec((1, N_EXPERTS))],
        out_specs=_tail_out_specs(bb, tt, d),
        scratch_shapes=[pltpu.VMEM((bb * tt, N_HEADS * V_DIM), BF16), pltpu.VMEM((1, N_EXPERTS), F32)],
        compiler_params=_cparams(("arbitrary", "arbitrary")),
        name="mla_post",
    )(h, o_lat, mod, w_uv, w_o, g_post, g_pre_ffn, w_router, b_router)


def _moe_layer(f, top_e, gates, rank, counts, h1, mod, g_post, wgu, bgu, wd, bd, bm, bb, tt):
    b, t, d = f.shape
    n = b * t
    n_assign = n * TOP_K
    n_blocks = (n_assign + N_EXPERTS * (bm - 1) + bm - 1) // bm
    counts = counts.reshape(N_EXPERTS).astype(jnp.int32)
    padded = (counts + bm - 1) // bm * bm
    pad_end = jnp.cumsum(padded)
    pad_start = pad_end - padded
    flat_e = top_e.reshape(n_assign)
    dest = pad_start[flat_e] + rank.reshape(n_assign)
    slot_tok = jnp.zeros((n_blocks * bm,), jnp.int32).at[dest].set(jnp.arange(n_assign, dtype=jnp.int32) // TOP_K)
    block_e = jnp.minimum(jnp.searchsorted(pad_end, jnp.arange(n_blocks, dtype=jnp.int32) * bm, side='right'),
                          N_EXPERTS - 1).astype(jnp.int32)
    nb_used = (pad_end[-1:] // bm).astype(jnp.int32)
    ys = _moe(f.reshape(n, d), slot_tok.reshape(n_blocks, bm), block_e, nb_used, wgu, bgu, wd, bd, bm)
    return _combine(ys, dest.astype(jnp.int32), gates, h1, mod, g_post, bb, tt)


def _rope_tables(pos):
    inv = jnp.exp(-jnp.log(ROPE_THETA) * jnp.arange(HALF_ROPE, dtype=jnp.float32) / HALF_ROPE)
    ang = pos.astype(jnp.float32)[:, None] * inv[None, :]
    cos, sin = jnp.cos(ang), jnp.sin(ang)
    ck = jnp.concatenate([cos, cos], axis=-1)
    sk = jnp.concatenate([-sin, sin], axis=-1)
    return ck, sk, jnp.tile(ck, (1, N_HEADS)), jnp.tile(sk, (1, N_HEADS))


def _trunk(x, mods, modkv, pos, prev_pad, attend, w, tiles):
    bb, tt, bm = tiles['bb'], tiles['tt'], tiles['bm']
    cbb, ctt = tiles['cbb'], tiles['ctt']
    row = lambda v: v.reshape(1, -1)
    h1, f, te, tg, tr, cnt, u_tail = _conv_stage(
        x, mods[0], prev_pad, row(w['g_pre_mix'][0]), w['w_conv_in'], w['w_conv_dw'], w['w_conv_out'],
        row(w['g_post_mix'][0]), row(w['g_pre_ffn'][0]), w['w_router'][0], row(w['b_router'][0]), bb, tt)
    h = _moe_layer(f, te, tg, tr, cnt, h1, mods[0], row(w['g_post_ffn'][0]),
                   w['w_gate_up'][0], w['b_gate_up'][0], w['w_down'][0], w['b_down'][0], bm, cbb, ctt)

    ck, sk, cq, sq = _rope_tables(pos)
    c_kv, k_rope, q_lat, q_rope = _mla_pre(
        h, mods[1], modkv, row(w['g_pre_mix'][1]), row(w['g_kv_in']), w['w_dkv'], row(w['g_kv_lat']), ck, sk,
        w['w_dq'], row(w['g_q']), w['w_uq'], w['w_ukt'], cq, sq, bb, tt)
    o_lat = attend(q_lat, q_rope, c_kv, k_rope)
    h1, f, te, tg, tr, cnt = _mla_post(
        h, o_lat, mods[1], w['w_uv'], w['w_o'], row(w['g_post_mix'][1]), row(w['g_pre_ffn'][1]),
        w['w_router'][1], row(w['b_router'][1]), bb, tt)
    y = _moe_layer(f, te, tg, tr, cnt, h1, mods[1], row(w['g_post_ffn'][1]),
                   w['w_gate_up'][1], w['b_gate_up'][1], w['w_down'][1], w['b_down'][1], bm, cbb, ctt)
    return y, u_tail[None, :, TAIL_ROWS - (CONV_W - 1):, :], c_kv, k_rope


def _prepare_weights(p):
    d = p['w_conv_in'].shape[1]
    w = dict(p)
    w['w_conv_in'] = p['w_conv_in'][0].astype(BF16)
    w['w_conv_dw'] = p['w_conv_dw'][0]
    w['w_conv_out'] = p['w_conv_out'][0].astype(BF16)
    wk = p['w_dkv']
    w['w_dkv'] = jnp.concatenate([wk, wk[:, KV_RANK + HALF_ROPE:], wk[:, KV_RANK:KV_RANK + HALF_ROPE]],
                                 axis=1).astype(BF16)
    wq = p['w_uq'][0].reshape(-1, N_HEADS, NOPE_DIM + ROPE_DIM)
    q_rank = wq.shape[0]
    nope = wq[:, :, :NOPE_DIM].reshape(q_rank, N_HEADS * NOPE_DIM)
    rope = wq[:, :, NOPE_DIM:]
    rope_sw = jnp.concatenate([rope[:, :, HALF_ROPE:], rope[:, :, :HALF_ROPE]], axis=-1)
    w['w_uq'] = jnp.concatenate([nope, rope.reshape(q_rank, -1), rope_sw.reshape(q_rank, -1)], axis=1).astype(BF16)
    w['w_dq'] = p['w_dq'][0].astype(BF16)
    w['g_q'] = p['g_q'][0]
    w['w_ukt'] = jnp.transpose(p['w_uk'], (1, 2, 0)).astype(BF16)
    w['w_uv'] = jnp.transpose(p['w_uv'], (1, 0, 2)).astype(BF16)
    w['w_o'] = p['w_o'][0].astype(BF16)
    w['w_router'] = p['w_router'].astype(BF16)
    w['w_gate_up'] = p['w_gate_up'].astype(BF16)
    w['w_down'] = p['w_down'].astype(BF16)
    w['b_gate_up'] = p['b_gate_up'][:, :, None, :]
    w['b_down'] = p['b_down'][:, :, None, :]
    del d
    return w


def kernel(x_prompt, x_sample, cache_kv_latent, cache_k_rope, state_conv, page_table, c_prompt, c_sample,
           w_ada, b_ada, g_pre_mix, g_post_mix, g_pre_ffn, g_post_ffn, w_conv_in, w_conv_dw, w_conv_out,
           w_ada_kv, b_ada_kv, g_kv_in, w_dkv, g_kv_lat, w_uk, w_uv, w_dq, g_q, w_uq, w_o,
           w_router, b_router, w_gate_up, b_gate_up, w_down, b_down):
    p = dict(g_pre_mix=g_pre_mix, g_post_mix=g_post_mix, g_pre_ffn=g_pre_ffn, g_post_ffn=g_post_ffn,
             w_conv_in=w_conv_in, w_conv_dw=w_conv_dw, w_conv_out=w_conv_out, g_kv_in=g_kv_in, w_dkv=w_dkv,
             g_kv_lat=g_kv_lat, w_uk=w_uk, w_uv=w_uv, w_dq=w_dq, g_q=g_q, w_uq=w_uq, w_o=w_o,
             w_router=w_router, b_router=b_router, w_gate_up=w_gate_up, b_gate_up=b_gate_up,
             w_down=w_down, b_down=b_down)
    w = _prepare_weights(p)
    bp, tp, d = x_prompt.shape
    bs, ts, _ = x_sample.shape
    depth = w_ada.shape[0]

    c_all = jnp.concatenate([c_prompt, c_sample], axis=0)
    mods = [_ada(c_all, w_ada[l].astype(BF16), b_ada[l]).reshape(bp + bs, 6, d) for l in range(depth)]
    modkv = _ada(c_all, w_ada_kv.astype(BF16), b_ada_kv).reshape(bp + bs, 2, d)

    pos_p = jnp.arange(tp, dtype=jnp.int32)
    prev_p = jnp.zeros((bp, TAIL_ROWS, d), F32)
    tiles_p = dict(bb=1, tt=min(tp, 512), bm=256, cbb=1, ctt=min(tp, 256))
    attend_p = lambda ql, qr, ckv, kr: _attn_prompt(ql, qr, ckv, kr, min(tp, 256))
    y_p, conv_p, lat_p, rope_p = _trunk(x_prompt, [m[:bp] for m in mods], modkv[:bp], pos_p, prev_p,
                                        attend_p, w, tiles_p)

    n_pages = page_table.shape[1]
    pos_s = n_pages * PAGE_SIZE + jnp.arange(ts, dtype=jnp.int32)
    prev_s = jnp.concatenate([jnp.zeros((bs, TAIL_ROWS - (CONV_W - 1), d), F32), state_conv[0]], axis=1)
    sbb = min(bs, 32)
    tiles_s = dict(bb=sbb, tt=ts, bm=128, cbb=sbb, ctt=ts)
    ppc = math.gcd(n_pages, 16)
    attend_s = lambda ql, qr, ckv, kr: _attn_paged(ql, qr, ckv, kr, cache_kv_latent, cache_k_rope, page_table, ppc)
    y_s, conv_s, lat_s, rope_s = _trunk(x_sample, [m[bp:] for m in mods], modkv[bp:], pos_s, prev_s,
                                        attend_s, w, tiles_s)
    return (y_p, y_s, lat_p, rope_p, conv_p, lat_s, rope_s, conv_s)
```

```python
import functools
import math

import jax
import jax.numpy as jnp
import numpy as np
from jax import lax
from jax.experimental import pallas as pl
from jax.experimental.pallas import tpu as pltpu

N_HEADS = 8
NOPE_DIM = 128
ROPE_DIM = 64
HALF_ROPE = ROPE_DIM // 2
V_DIM = 128
KV_RANK = 256
ROPE_THETA = 10000.0
SM_SCALE = (NOPE_DIM + ROPE_DIM) ** -0.5
N_EXPERTS = 32
TOP_K = 4
SWIGLU_LIMIT = 7.0
SWIGLU_ALPHA = 1.702
RMS_EPS = 1e-6
NEG_INF = -1e30
PAGE_SIZE = 128
CONV_W = 3
TAIL_ROWS = 8

VMEM_LIMIT = 56 * 1024 * 1024
BF16 = jnp.bfloat16
F32 = jnp.float32


def _cparams(sem):
    return pltpu.CompilerParams(dimension_semantics=sem, vmem_limit_bytes=VMEM_LIMIT)


def _const_spec(shape):
    nd = len(shape)
    return pl.BlockSpec(shape, lambda *_: (0,) * nd)


def _rms(x, g):
    ms = jnp.mean(x * x, axis=-1, keepdims=True)
    return x * lax.rsqrt(ms + RMS_EPS) * g


def _dot(a, b):
    return jnp.dot(a.astype(BF16), b.astype(BF16), preferred_element_type=F32)


def _dot_nt(a, b):
    return lax.dot_general(a.astype(BF16), b.astype(BF16), (((1,), (1,)), ((), ())),
                           preferred_element_type=F32)


def _modvec(mod_ref, j, bb, tt):
    v = mod_ref[:, j:j + 1, :]
    d = v.shape[-1]
    if bb == 1:
        return v[0]
    return jnp.broadcast_to(v, (bb, tt, d)).reshape(bb * tt, d)


def _ada_kernel(c_ref, w_ref, b_ref, o_ref):
    c = c_ref[...]
    c_act = c * jax.nn.sigmoid(c)
    o_ref[...] = _dot(c_act, w_ref[...]) + b_ref[...]


def _ada(c, w, b, tn=1024):
    m, d = c.shape
    n = w.shape[1]
    return pl.pallas_call(
        _ada_kernel,
        out_shape=jax.ShapeDtypeStruct((m, n), F32),
        grid=(n // tn,),
        in_specs=[_const_spec((m, d)), pl.BlockSpec((d, tn), lambda j: (0, j)),
                  pl.BlockSpec((1, tn), lambda j: (0, j))],
        out_specs=pl.BlockSpec((m, tn), lambda j: (0, j)),
        compiler_params=_cparams(("arbitrary",)),
        name="ada",
    )(c, w, b.reshape(1, n))


def _stage_tail(first_step, h, mix, gt_m, sh_f, sc_f, g_post, g_pre_ffn, w_router, b_router,
                cnt_ref, h_out, f_out, e_out, g_out, r_out, c_out, bb, tt):
    rows = bb * tt
    h1 = h + gt_m * _rms(mix, g_post)
    f = _rms(h1, g_pre_ffn) * (1.0 + sc_f) + sh_f
    h_out[...] = h1.reshape(bb, tt, -1)
    f_out[...] = f.reshape(bb, tt, -1)

    logits = _dot(f, w_router) + b_router
    lane = lax.broadcasted_iota(jnp.int32, logits.shape, 1)
    lane_k = lax.broadcasted_iota(jnp.int32, (rows, TOP_K), 1)
    work = logits
    multi_hot = jnp.zeros(logits.shape, F32)
    top_val, top_idx, top_sel = [], [], []
    for _ in range(TOP_K):
        m = jnp.max(work, axis=-1, keepdims=True)
        idx = jnp.min(jnp.where(work == m, lane, N_EXPERTS), axis=-1, keepdims=True)
        sel = lane == idx
        multi_hot = multi_hot + sel.astype(F32)
        work = jnp.where(sel, -jnp.inf, work)
        top_val.append(m)
        top_idx.append(idx)
        top_sel.append(sel)
    exps = [jnp.exp(v - top_val[0]) for v in top_val]
    denom = exps[0] + exps[1] + exps[2] + exps[3]

    @pl.when(first_step)
    def _():
        cnt_ref[...] = jnp.zeros_like(cnt_ref)

    ri = lax.broadcasted_iota(jnp.int32, (rows, rows), 0)
    ci = lax.broadcasted_iota(jnp.int32, (rows, rows), 1)
    strict_lower = (ri > ci).astype(BF16)
    before = jnp.dot(strict_lower, multi_hot.astype(BF16), preferred_element_type=F32) + cnt_ref[...]

    e_val = jnp.zeros((rows, TOP_K), jnp.int32)
    g_val = jnp.zeros((rows, TOP_K), F32)
    r_val = jnp.zeros((rows, TOP_K), F32)
    for k in range(TOP_K):
        rank_k = jnp.sum(jnp.where(top_sel[k], before, 0.0), axis=-1, keepdims=True)
        e_val = jnp.where(lane_k == k, top_idx[k], e_val)
        g_val = jnp.where(lane_k == k, exps[k] / denom, g_val)
        r_val = jnp.where(lane_k == k, rank_k, r_val)
    e_out[...] = e_val.reshape(bb, tt, TOP_K)
    g_out[...] = g_val.reshape(bb, tt, TOP_K)
    r_out[...] = r_val.astype(jnp.int32).reshape(bb, tt, TOP_K)
    cnt_ref[...] = cnt_ref[...] + jnp.sum(multi_hot, axis=0, keepdims=True)
    c_out[...] = cnt_ref[...]


def _tail_out_shapes(b, t, d):
    return [jax.ShapeDtypeStruct((b, t, d), F32), jax.ShapeDtypeStruct((b, t, d), F32),
            jax.ShapeDtypeStruct((b, t, TOP_K), jnp.int32), jax.ShapeDtypeStruct((b, t, TOP_K), F32),
            jax.ShapeDtypeStruct((b, t, TOP_K), jnp.int32), jax.ShapeDtypeStruct((1, N_EXPERTS), F32)]


def _tail_out_specs(bb, tt, d):
    tok = lambda n: pl.BlockSpec((bb, tt, n), lambda i, j: (i, j, 0))
    return [tok(d), tok(d), tok(TOP_K), tok(TOP_K), tok(TOP_K),
            pl.BlockSpec((1, N_EXPERTS), lambda i, j: (0, 0))]


def _conv_stage_kernel(x_ref, mod_ref, prev_ref, g_pre, w_in, w_dw, w_out, g_post, g_pre_ffn,
                       w_router, b_router,
                       h_out, f_out, e_out, g_out, r_out, c_out, u_tail,
                       carry, cnt_ref, *, bb, tt, d):
    rows = bb * tt
    i, j = pl.program_id(0), pl.program_id(1)
    x = x_ref[...].reshape(rows, d)
    mv = lambda k: _modvec(mod_ref, k, bb, tt)
    a = _rms(x, g_pre[...]) * (1.0 + mv(1)) + mv(0)
    bcv = _dot(a, w_in[...])
    b_gate, c_gate, v = bcv[:, :d], bcv[:, d:2 * d], bcv[:, 2 * d:]
    u = c_gate * v

    tpos = lax.broadcasted_iota(jnp.int32, (rows, 1), 0) % tt
    if bb == 1:
        @pl.when(j == 0)
        def _():
            carry[...] = prev_ref[0]
        p0, p1 = carry[TAIL_ROWS - 2:TAIL_ROWS - 1, :], carry[TAIL_ROWS - 1:TAIL_ROWS, :]
        u_m1 = jnp.where(tpos == 0, p1, pltpu.roll(u, 1, 0))
        u_m2 = jnp.where(tpos == 0, p0, jnp.where(tpos == 1, p1, pltpu.roll(u, 2, 0)))
        carry[...] = u[rows - TAIL_ROWS:, :]
        u_tail[0] = u[rows - TAIL_ROWS:, :]
    else:
        pp = prev_ref[...].reshape(rows, d)
        u_m1 = jnp.where(tpos == 0, pltpu.roll(pp, rows - (TAIL_ROWS - 1), 0), pltpu.roll(u, 1, 0))
        u_m2 = jnp.where(tpos < 2, pltpu.roll(pp, rows - (TAIL_ROWS - 2), 0), pltpu.roll(u, 2, 0))
        u_tail[...] = u.reshape(bb, tt, d)
    conv = w_dw[0:1, :] * u_m2 + w_dw[1:2, :] * u_m1 + w_dw[2:3, :] * u
    mix = _dot(b_gate * conv, w_out[...])

    _stage_tail(jnp.logical_and(i == 0, j == 0), x, mix, mv(2), mv(3), mv(4), g_post[...], g_pre_ffn[...],
                w_router[...], b_router[...], cnt_ref, h_out, f_out, e_out, g_out, r_out, c_out, bb, tt)


def _conv_stage(x, mod, prev_pad, g_pre, w_in, w_dw, w_out, g_post, g_pre_ffn, w_router, b_router, bb, tt):
    b, t, d = x.shape
    assert b % bb == 0 and t % tt == 0 and (bb == 1 or tt == t == TAIL_ROWS)
    tok = lambda n: pl.BlockSpec((bb, tt, n), lambda i, j: (i, j, 0))
    seq = lambda n: pl.BlockSpec((bb, n, d), lambda i, j: (i, 0, 0))
    kern = functools.partial(_conv_stage_kernel, bb=bb, tt=tt, d=d)
    outs = pl.pallas_call(
        kern,
        out_shape=_tail_out_shapes(b, t, d) + [jax.ShapeDtypeStruct((b, TAIL_ROWS, d), F32)],
        grid=(b // bb, t // tt),
        in_specs=[tok(d), seq(6), seq(TAIL_ROWS), _const_spec((1, d)), _const_spec((d, 3 * d)),
                  _const_spec((CONV_W, d)), _const_spec((d, d)), _const_spec((1, d)), _const_spec((1, d)),
                  _const_spec((d, N_EXPERTS)), _const_spec((1, N_EXPERTS))],
        out_specs=_tail_out_specs(bb, tt, d) + [seq(TAIL_ROWS)],
        scratch_shapes=[pltpu.VMEM((TAIL_ROWS, d), F32), pltpu.VMEM((1, N_EXPERTS), F32)],
        compiler_params=_cparams(("arbitrary", "arbitrary")),
        name="conv_stage",
    )(x, mod, prev_pad, g_pre, w_in, w_dw, w_out, g_post, g_pre_ffn, w_router, b_router)
    return outs


def _moe_kernel(tok_ref, be_ref, x_hbm, wgu, bgu, wd, bd, o_ref, xbuf, wgu_b, wd_b, sem, *, bm, d_ff):
    i = pl.program_id(0)
    last = pl.num_programs(0) - 1
    slot = i % 2

    def issue(blk, s):
        for r in range(bm):
            pltpu.make_async_copy(x_hbm.at[pl.ds(tok_ref[blk, r], 1)], xbuf.at[s, pl.ds(r, 1)], sem.at[s]).start()

    def wait(s):
        pltpu.make_async_copy(x_hbm.at[pl.ds(0, bm)], xbuf.at[s], sem.at[s]).wait()

    @pl.when(i == 0)
    def _():
        issue(0, 0)

    @pl.when(jnp.logical_or(i == 0, be_ref[i] != be_ref[jnp.maximum(i - 1, 0)]))
    def _():
        wgu_b[...] = wgu[0, 0].astype(BF16)
        wd_b[...] = wd[0, 0].astype(BF16)

    wait(slot)
    issue(jnp.minimum(i + 1, last), 1 - slot)
    x = xbuf[slot]
    hgu = _dot(x, wgu_b[...]) + bgu[0, 0]
    gate = jnp.minimum(hgu[:, :d_ff], SWIGLU_LIMIT)
    up = jnp.clip(hgu[:, d_ff:], -SWIGLU_LIMIT, SWIGLU_LIMIT)
    glu = gate * jax.nn.sigmoid(gate * SWIGLU_ALPHA)
    o_ref[...] = _dot((up + 1.0) * glu, wd_b[...]) + bd[0, 0]

    @pl.when(i == last)
    def _():
        wait(1 - slot)


def _moe(x2d, slot_tok, block_e, layer, wgu, bgu, wd, bd, bm):
    n, d = x2d.shape
    n_blocks = slot_tok.shape[0]
    d_ff = wd.shape[2]
    kern = functools.partial(_moe_kernel, bm=bm, d_ff=d_ff)
    grid_spec = pltpu.PrefetchScalarGridSpec(
        num_scalar_prefetch=2,
        grid=(n_blocks,),
        in_specs=[pl.BlockSpec(memory_space=pl.ANY),
                  pl.BlockSpec((1, 1, d, 2 * d_ff), lambda i, t, e: (layer, e[i], 0, 0)),
                  pl.BlockSpec((1, 1, 1, 2 * d_ff), lambda i, t, e: (layer, e[i], 0, 0)),
                  pl.BlockSpec((1, 1, d_ff, d), lambda i, t, e: (layer, e[i], 0, 0)),
                  pl.BlockSpec((1, 1, 1, d), lambda i, t, e: (layer, e[i], 0, 0))],
        out_specs=pl.BlockSpec((bm, d), lambda i, t, e: (i, 0)),
        scratch_shapes=[pltpu.VMEM((2, bm, d), F32), pltpu.VMEM((d, 2 * d_ff), BF16),
                        pltpu.VMEM((d_ff, d), BF16), pltpu.SemaphoreType.DMA((2,))],
    )
    return pl.pallas_call(
        kern,
        out_shape=jax.ShapeDtypeStruct((n_blocks * bm, d), F32),
        grid_spec=grid_spec,
        compiler_params=_cparams(("arbitrary",)),
        name="moe",
    )(slot_tok, block_e, x2d, wgu, bgu, wd, bd)


def _combine_kernel(dest_ref, y_hbm, gates_ref, h_ref, mod_ref, g_post, o_ref, ybuf, sem, *, bb, tt, d, n_j):
    rows = bb * tt
    i, j = pl.program_id(0), pl.program_id(1)
    step = i * n_j + j
    n_steps = pl.num_programs(0) * n_j
    slot = step % 2

    def issue(st, s):
        for r in range(rows):
            for k in range(TOP_K):
                pltpu.make_async_copy(y_hbm.at[pl.ds(dest_ref[st, r * TOP_K + k], 1)],
                                      ybuf.at[s, k, pl.ds(r, 1)], sem.at[s]).start()

    def wait(s):
        for k in range(TOP_K):
            pltpu.make_async_copy(y_hbm.at[pl.ds(0, rows)], ybuf.at[s, k], sem.at[s]).wait()

    @pl.when(step == 0)
    def _():
        issue(0, 0)

    wait(slot)
    issue(jnp.minimum(step + 1, n_steps - 1), 1 - slot)
    g = gates_ref[...].reshape(rows, TOP_K)
    f = g[:, 0:1] * ybuf[slot, 0]
    for k in range(1, TOP_K):
        f = f + g[:, k:k + 1] * ybuf[slot, k]
    h = h_ref[...].reshape(rows, d)
    gt_f = _modvec(mod_ref, 5, bb, tt)
    o_ref[...] = (h + gt_f * _rms(f, g_post[...])).reshape(bb, tt, d)

    @pl.when(step == n_steps - 1)
    def _():
        wait(1 - slot)


def _combine(ys, dest, gates, h1, mod, g_post, bb, tt):
    b, t, d = h1.shape
    n_j = t // tt
    rows = bb * tt
    n_steps = (b // bb) * n_j
    dest2d = dest.reshape(n_steps, rows * TOP_K)
    tok = lambda n: pl.BlockSpec((bb, tt, n), lambda i, j, dr: (i, j, 0))
    kern = functools.partial(_combine_kernel, bb=bb, tt=tt, d=d, n_j=n_j)
    grid_spec = pltpu.PrefetchScalarGridSpec(
        num_scalar_prefetch=1,
        grid=(b // bb, n_j),
        in_specs=[pl.BlockSpec(memory_space=pl.ANY), tok(TOP_K), tok(d),
                  pl.BlockSpec((bb, 6, d), lambda i, j, dr: (i, 0, 0)),
                  pl.BlockSpec((1, d), lambda i, j, dr: (0, 0))],
        out_specs=tok(d),
        scratch_shapes=[pltpu.VMEM((2, TOP_K, rows, d), F32), pltpu.SemaphoreType.DMA((2,))],
    )
    return pl.pallas_call(
        kern,
        out_shape=jax.ShapeDtypeStruct((b, t, d), F32),
        grid_spec=grid_spec,
        compiler_params=_cparams(("arbitrary", "arbitrary")),
        name="combine",
    )(dest2d, ys, gates, h1, mod, g_post)


def _mla_pre_kernel(h_ref, mod_ref, modkv_ref, g_pre, g_kv_in, w_dkv, g_kv_lat, ck_ref, sk_ref,
                    w_dq, g_q, w_uq, w_ukt, cq_ref, sq_ref,
                    ckv_out, kr_out, ql_out, qr_out, *, bb, tt, d):
    rows = bb * tt
    h = h_ref[...].reshape(rows, d)
    ms = jnp.mean(h * h, axis=-1, keepdims=True)
    hn = h * lax.rsqrt(ms + RMS_EPS)

    kv_in = (hn * g_kv_in[...]) * (1.0 + _modvec(modkv_ref, 1, bb, tt)) + _modvec(modkv_ref, 0, bb, tt)
    lat = _dot(kv_in, w_dkv[...])
    ckv_out[...] = _rms(lat[:, :KV_RANK], g_kv_lat[...]).reshape(bb, tt, KV_RANK)
    kx = lat[:, KV_RANK:KV_RANK + ROPE_DIM].reshape(bb, tt, ROPE_DIM)
    kxs = lat[:, KV_RANK + ROPE_DIM:].reshape(bb, tt, ROPE_DIM)
    kr_out[...] = kx * ck_ref[...][None] + kxs * sk_ref[...][None]

    a = (hn * g_pre[...]) * (1.0 + _modvec(mod_ref, 1, bb, tt)) + _modvec(mod_ref, 0, bb, tt)
    qc = _rms(_dot(a, w_dq[...]), g_q[...])
    q = _dot(qc, w_uq[...])
    n_nope, n_rope = N_HEADS * NOPE_DIM, N_HEADS * ROPE_DIM
    qx = q[:, n_nope:n_nope + n_rope].reshape(bb, tt, n_rope)
    qxs = q[:, n_nope + n_rope:].reshape(bb, tt, n_rope)
    q_rope = qx * cq_ref[...][None] + qxs * sq_ref[...][None]
    for hd in range(N_HEADS):
        q_lat = _dot(q[:, hd * NOPE_DIM:(hd + 1) * NOPE_DIM], w_ukt[hd])
        ql_out[:, hd] = q_lat.reshape(bb, tt, KV_RANK).astype(ql_out.dtype)
        qr_out[:, hd] = q_rope[:, :, hd * ROPE_DIM:(hd + 1) * ROPE_DIM].astype(qr_out.dtype)


def _mla_pre(h, mod, modkv, g_pre, g_kv_in, w_dkv, g_kv_lat, ck, sk, w_dq, g_q, w_uq, w_ukt, cq, sq, bb, tt,
             q_dtype):
    b, t, d = h.shape
    q_rank = w_dq.shape[1]
    tok = lambda n: pl.BlockSpec((bb, tt, n), lambda i, j: (i, j, 0))
    head = lambda n: pl.BlockSpec((bb, N_HEADS, tt, n), lambda i, j: (i, 0, j, 0))
    seq = lambda n: pl.BlockSpec((bb, n, d), lambda i, j: (i, 0, 0))
    pos = lambda n: pl.BlockSpec((tt, n), lambda i, j: (j, 0))
    kern = functools.partial(_mla_pre_kernel, bb=bb, tt=tt, d=d)
    return pl.pallas_call(
        kern,
        out_shape=[jax.ShapeDtypeStruct((b, t, KV_RANK), F32), jax.ShapeDtypeStruct((b, t, ROPE_DIM), F32),
                   jax.ShapeDtypeStruct((b, N_HEADS, t, KV_RANK), q_dtype),
                   jax.ShapeDtypeStruct((b, N_HEADS, t, ROPE_DIM), q_dtype)],
        grid=(b // bb, t // tt),
        in_specs=[tok(d), seq(6), seq(2), _const_spec((1, d)), _const_spec((1, d)),
                  _const_spec(w_dkv.shape), _const_spec((1, KV_RANK)), pos(ROPE_DIM), pos(ROPE_DIM),
                  _const_spec(w_dq.shape), _const_spec((1, q_rank)), _const_spec(w_uq.shape),
                  _const_spec(w_ukt.shape), pos(N_HEADS * ROPE_DIM), pos(N_HEADS * ROPE_DIM)],
        out_specs=[tok(KV_RANK), tok(ROPE_DIM), head(KV_RANK), head(ROPE_DIM)],
        compiler_params=_cparams(("arbitrary", "arbitrary")),
        name="mla_pre",
    )(h, mod, modkv, g_pre, g_kv_in, w_dkv, g_kv_lat, ck, sk, w_dq, g_q, w_uq, w_ukt, cq, sq)


LANES = 128


def _softmax_step(s, m_prev, l_prev):
    keys = s.shape[1]
    m_new = jnp.maximum(m_prev, jnp.max(s, axis=-1, keepdims=True))
    alpha = jnp.exp(m_prev - m_new)
    p = jnp.exp(s - (jnp.tile(m_new, (1, keys // LANES)) if keys >= LANES else m_new[:, :keys]))
    l_new = alpha * l_prev + jnp.sum(p, axis=-1, keepdims=True)
    return p, m_new, l_new, alpha


def _attn_prompt_kernel(ql_ref, qr_ref, ckv_ref, kr_ref, o_ref, m_ref, l_ref, acc_ref, *, tq):
    i, j = pl.program_id(1), pl.program_id(2)
    rows = N_HEADS * tq

    @pl.when(j == 0)
    def _():
        m_ref[...] = jnp.full_like(m_ref, -jnp.inf)
        l_ref[...] = jnp.zeros_like(l_ref)
        acc_ref[...] = jnp.zeros_like(acc_ref)

    def step(masked):
        ckv = ckv_ref[0].astype(BF16)
        kr = kr_ref[0].astype(BF16)
        q = ql_ref[0].reshape(rows, KV_RANK)
        qr = qr_ref[0].reshape(rows, ROPE_DIM)
        s = (_dot_nt(q, ckv) + _dot_nt(qr, kr)) * SM_SCALE
        if masked:
            q_pos = lax.broadcasted_iota(jnp.int32, (rows, tq), 0) % tq
            k_pos = lax.broadcasted_iota(jnp.int32, (rows, tq), 1)
            s = jnp.where(k_pos <= q_pos, s, NEG_INF)
        p, m_new, l_new, alpha = _softmax_step(s, m_ref[...], l_ref[...])
        m_ref[...] = m_new
        l_ref[...] = l_new
        acc_ref[...] = (jnp.tile(alpha, (1, KV_RANK // LANES)) * acc_ref[...]
                        + jnp.dot(p.astype(BF16), ckv, preferred_element_type=F32))

    @pl.when(j < i)
    def _():
        step(False)

    @pl.when(j == i)
    def _():
        step(True)
        out = acc_ref[...] / jnp.tile(l_ref[...], (1, KV_RANK // LANES))
        o_ref[0] = out.reshape(N_HEADS, tq, KV_RANK).astype(o_ref.dtype)


def _attn_prompt(q_lat, q_rope, c_kv, k_rope, tq):
    b, _, t, _ = q_lat.shape
    n = t // tq
    rows = N_HEADS * tq
    qspec = lambda w: pl.BlockSpec((1, N_HEADS, tq, w), lambda bi, i, j: (bi, 0, i, 0))
    kspec = lambda w: pl.BlockSpec((1, tq, w), lambda bi, i, j: (bi, jnp.minimum(i, j), 0))
    return pl.pallas_call(
        functools.partial(_attn_prompt_kernel, tq=tq),
        out_shape=jax.ShapeDtypeStruct((b, N_HEADS, t, KV_RANK), BF16),
        grid=(b, n, n),
        in_specs=[qspec(KV_RANK), qspec(ROPE_DIM), kspec(KV_RANK), kspec(ROPE_DIM)],
        out_specs=qspec(KV_RANK),
        scratch_shapes=[pltpu.VMEM((rows, LANES), F32), pltpu.VMEM((rows, LANES), F32),
                        pltpu.VMEM((rows, KV_RANK), F32)],
        compiler_params=_cparams(("arbitrary", "arbitrary", "arbitrary")),
        name="attn_prompt",
    )(q_lat, q_rope, c_kv, k_rope)


def _attn_paged_kernel(pt_ref, ql_ref, qr_ref, cnew_ref, krnew_ref, *rest, ppc, spb, t_new):
    lat_pages = rest[:spb * ppc]
    rope_pages = rest[spb * ppc:2 * spb * ppc]
    o_ref = rest[2 * spb * ppc]
    kbuf, m_ref, l_ref, acc_ref = rest[2 * spb * ppc + 1:]
    j = pl.program_id(1)
    rows = N_HEADS * t_new

    @pl.when(j == 0)
    def _():
        m_ref[...] = jnp.full_like(m_ref, -jnp.inf)
        l_ref[...] = jnp.zeros_like(l_ref)
        acc_ref[...] = jnp.zeros_like(acc_ref)

    for sq in range(spb):
        q = ql_ref[sq].reshape(rows, KV_RANK).astype(BF16)
        qr = qr_ref[sq].reshape(rows, ROPE_DIM).astype(BF16)
        s_pages = []
        for k in range(ppc):
            kb = lat_pages[sq * ppc + k][0].astype(BF16)
            kbuf[sq, k * PAGE_SIZE:(k + 1) * PAGE_SIZE, :] = kb
            s_pages.append(_dot_nt(q, kb) + _dot(qr, rope_pages[sq * ppc + k][0]))
        s = jnp.concatenate(s_pages, axis=1) * SM_SCALE
        p, m_new, l_new, alpha = _softmax_step(s, m_ref[sq], l_ref[sq])
        m_ref[sq] = m_new
        l_ref[sq] = l_new
        acc_ref[sq] = (jnp.tile(alpha, (1, KV_RANK // LANES)) * acc_ref[sq]
                       + jnp.dot(p.astype(BF16), kbuf[sq], preferred_element_type=F32))

    @pl.when(j == pl.num_programs(1) - 1)
    def _():
        for sq in range(spb):
            q = ql_ref[sq].reshape(rows, KV_RANK).astype(BF16)
            qr = qr_ref[sq].reshape(rows, ROPE_DIM).astype(BF16)
            c_new = cnew_ref[sq].astype(BF16)
            s_new = (_dot_nt(q, c_new) + _dot_nt(qr, krnew_ref[sq])) * SM_SCALE
            q_pos = lax.broadcasted_iota(jnp.int32, (rows, t_new), 0) % t_new
            k_pos = lax.broadcasted_iota(jnp.int32, (rows, t_new), 1)
            s_new = jnp.where(k_pos <= q_pos, s_new, NEG_INF)
            p, m_new, l_new, alpha = _softmax_step(s_new, m_ref[sq], l_ref[sq])
            acc = (jnp.tile(alpha, (1, KV_RANK // LANES)) * acc_ref[sq]
                   + jnp.dot(p.astype(BF16), c_new, preferred_element_type=F32))
            out = acc / jnp.tile(l_new, (1, KV_RANK // LANES))
            o_ref[sq] = out.reshape(N_HEADS, t_new, KV_RANK)


def _attn_paged(q_lat, q_rope, c_new, kr_new, cache_lat, cache_rope_t, page_table, ppc, spb):
    s, _, t_new, _ = q_lat.shape
    n_pages = page_table.shape[1]
    assert n_pages % ppc == 0 and s % spb == 0
    rows = N_HEADS * t_new
    head = lambda w: pl.BlockSpec((spb, N_HEADS, t_new, w), lambda b, j, pt: (b, 0, 0, 0))
    seq = lambda w: pl.BlockSpec((spb, t_new, w), lambda b, j, pt: (b, 0, 0))
    lat_page = lambda sq, k: pl.BlockSpec((1, PAGE_SIZE, KV_RANK),
                                          lambda b, j, pt: (pt[b * spb + sq, j * ppc + k], 0, 0))
    rope_page = lambda sq, k: pl.BlockSpec((1, ROPE_DIM, PAGE_SIZE),
                                           lambda b, j, pt: (pt[b * spb + sq, j * ppc + k], 0, 0))
    pairs = [(sq, k) for sq in range(spb) for k in range(ppc)]
    grid_spec = pltpu.PrefetchScalarGridSpec(
        num_scalar_prefetch=1,
        grid=(s // spb, n_pages // ppc),
        in_specs=[head(KV_RANK), head(ROPE_DIM), seq(KV_RANK), seq(ROPE_DIM)]
                 + [lat_page(sq, k) for sq, k in pairs] + [rope_page(sq, k) for sq, k in pairs],
        out_specs=head(KV_RANK),
        scratch_shapes=[pltpu.VMEM((spb, ppc * PAGE_SIZE, KV_RANK), BF16),
                        pltpu.VMEM((spb, rows, LANES), F32), pltpu.VMEM((spb, rows, LANES), F32),
                        pltpu.VMEM((spb, rows, KV_RANK), F32)],
    )
    n_in = spb * ppc
    return pl.pallas_call(
        functools.partial(_attn_paged_kernel, ppc=ppc, spb=spb, t_new=t_new),
        out_shape=jax.ShapeDtypeStruct((s, N_HEADS, t_new, KV_RANK), F32),
        grid_spec=grid_spec,
        compiler_params=_cparams(("arbitrary", "arbitrary")),
        name="attn_paged",
    )(page_table, q_lat, q_rope, c_new, kr_new, *([cache_lat] * n_in), *([cache_rope_t] * n_in))


def _mla_post_kernel(h_ref, ol_ref, mod_ref, w_uv, w_o, g_post, g_pre_ffn, w_router, b_router,
                     h_out, f_out, e_out, g_out, r_out, c_out, obuf, cnt_ref, *, bb, tt, d):
    rows = bb * tt
    i, j = pl.program_id(0), pl.program_id(1)
    h = h_ref[...].reshape(rows, d)
    for hd in range(N_HEADS):
        o_lat = ol_ref[:, hd].reshape(rows, KV_RANK)
        obuf[:, hd * V_DIM:(hd + 1) * V_DIM] = _dot(o_lat, w_uv[hd]).astype(BF16)
    mix = jnp.dot(obuf[...], w_o[...], preferred_element_type=F32)
    mv = lambda k: _modvec(mod_ref, k, bb, tt)
    _stage_tail(jnp.logical_and(i == 0, j == 0), h, mix, mv(2), mv(3), mv(4), g_post[...], g_pre_ffn[...],
                w_router[...], b_router[...], cnt_ref, h_out, f_out, e_out, g_out, r_out, c_out, bb, tt)


def _mla_post(h, o_lat, mod, w_uv, w_o, g_post, g_pre_ffn, w_router, b_router, bb, tt):
    b, t, d = h.shape
    tok = lambda n: pl.BlockSpec((bb, tt, n), lambda i, j: (i, j, 0))
    kern = functools.partial(_mla_post_kernel, bb=bb, tt=tt, d=d)
    return pl.pallas_call(
        kern,
        out_shape=_tail_out_shapes(b, t, d),
        grid=(b // bb, t // tt),
        in_specs=[tok(d), pl.BlockSpec((bb, N_HEADS, tt, KV_RANK), lambda i, j: (i, 0, j, 0)),
                  pl.BlockSpec((bb, 6, d), lambda i, j: (i, 0, 0)),
                  _const_spec(w_uv.shape), _const_spec(w_o.shape), _const_spec((1, d)), _const_spec((1, d)),
                  _const_spec((d, N_EXPERTS)), _const_spec((1, N_EXPERTS))],
        out_specs=_tail_out_specs(bb, tt, d),
        scratch_shapes=[pltpu.VMEM((bb * tt, N_HEADS * V_DIM), BF16), pltpu.VMEM((1, N_EXPERTS), F32)],
        compiler_params=_cparams(("arbitrary", "arbitrary")),
        name="mla_post",
    )(h, o_lat, mod, w_uv, w_o, g_post, g_pre_ffn, w_router, b_router)


def _moe_layer(f, top_e, gates, rank, counts, h1, mod, g_post, layer, wgu, bgu, wd, bd, bm, bb, tt):
    b, t, d = f.shape
    n = b * t
    n_assign = n * TOP_K
    n_blocks = (n_assign + N_EXPERTS * (bm - 1) + bm - 1) // bm
    counts = counts.reshape(N_EXPERTS).astype(jnp.int32)
    padded = (counts + bm - 1) // bm * bm
    pad_end = jnp.cumsum(padded)
    pad_start = pad_end - padded
    flat_e = top_e.reshape(n_assign)
    dest = pad_start[flat_e] + rank.reshape(n_assign)
    slot_tok = jnp.zeros((n_blocks * bm,), jnp.int32).at[dest].set(jnp.arange(n_assign, dtype=jnp.int32) // TOP_K)
    block_start = jnp.arange(n_blocks, dtype=jnp.int32) * bm
    block_e = jnp.minimum(jnp.sum(pad_end[None, :] <= block_start[:, None], axis=1), N_EXPERTS - 1).astype(jnp.int32)
    ys = _moe(f.reshape(n, d), slot_tok.reshape(n_blocks, bm), block_e, layer, wgu, bgu, wd, bd, bm)
    return _combine(ys, dest.astype(jnp.int32), gates, h1, mod, g_post, bb, tt)


def _rope_tables(pos):
    inv = jnp.exp(-jnp.log(ROPE_THETA) * jnp.arange(HALF_ROPE, dtype=jnp.float32) / HALF_ROPE)
    ang = pos.astype(jnp.float32)[:, None] * inv[None, :]
    cos, sin = jnp.cos(ang), jnp.sin(ang)
    ck = jnp.concatenate([cos, cos], axis=-1)
    sk = jnp.concatenate([-sin, sin], axis=-1)
    return ck, sk, jnp.tile(ck, (1, N_HEADS)), jnp.tile(sk, (1, N_HEADS))


def _trunk(x, mods, modkv, pos, prev_pad, attend, w, tiles):
    bb, tt, bm = tiles['bb'], tiles['tt'], tiles['bm']
    cbb, ctt = tiles['cbb'], tiles['ctt']
    row = lambda v: v.reshape(1, -1)
    h1, f, te, tg, tr, cnt, u_tail = _conv_stage(
        x, mods[0], prev_pad, row(w['g_pre_mix'][0]), w['w_conv_in'], w['w_conv_dw'], w['w_conv_out'],
        row(w['g_post_mix'][0]), row(w['g_pre_ffn'][0]), w['w_router'][0], row(w['b_router'][0]), bb, tt)
    h = _moe_layer(f, te, tg, tr, cnt, h1, mods[0], row(w['g_post_ffn'][0]), 0,
                   w['w_gate_up'], w['b_gate_up'], w['w_down'], w['b_down'], bm, cbb, ctt)

    ck, sk, cq, sq = _rope_tables(pos)
    c_kv, k_rope, q_lat, q_rope = _mla_pre(
        h, mods[1], modkv, row(w['g_pre_mix'][1]), row(w['g_kv_in']), w['w_dkv'], row(w['g_kv_lat']), ck, sk,
        w['w_dq'], row(w['g_q']), w['w_uq'], w['w_ukt'], cq, sq, bb, tt, tiles['q_dtype'])
    o_lat = attend(q_lat, q_rope, c_kv, k_rope)
    h1, f, te, tg, tr, cnt = _mla_post(
        h, o_lat, mods[1], w['w_uv'], w['w_o'], row(w['g_post_mix'][1]), row(w['g_pre_ffn'][1]),
        w['w_router'][1], row(w['b_router'][1]), bb, tt)
    y = _moe_layer(f, te, tg, tr, cnt, h1, mods[1], row(w['g_post_ffn'][1]), 1,
                   w['w_gate_up'], w['b_gate_up'], w['w_down'], w['b_down'], bm, cbb, ctt)
    return y, u_tail[None, :, TAIL_ROWS - (CONV_W - 1):, :], c_kv, k_rope


def _prepare_weights(p):
    w = dict(p)
    w['w_conv_in'] = p['w_conv_in'][0].astype(BF16)
    w['w_conv_dw'] = p['w_conv_dw'][0]
    w['w_conv_out'] = p['w_conv_out'][0].astype(BF16)
    wk = p['w_dkv']
    w['w_dkv'] = jnp.concatenate([wk, wk[:, KV_RANK + HALF_ROPE:], wk[:, KV_RANK:KV_RANK + HALF_ROPE]],
                                 axis=1).astype(BF16)
    wq = p['w_uq'][0].reshape(-1, N_HEADS, NOPE_DIM + ROPE_DIM)
    q_rank = wq.shape[0]
    nope = wq[:, :, :NOPE_DIM].reshape(q_rank, N_HEADS * NOPE_DIM)
    rope = wq[:, :, NOPE_DIM:]
    rope_sw = jnp.concatenate([rope[:, :, HALF_ROPE:], rope[:, :, :HALF_ROPE]], axis=-1)
    w['w_uq'] = jnp.concatenate([nope, rope.reshape(q_rank, -1), rope_sw.reshape(q_rank, -1)], axis=1).astype(BF16)
    w['w_dq'] = p['w_dq'][0].astype(BF16)
    w['g_q'] = p['g_q'][0]
    w['w_ukt'] = jnp.transpose(p['w_uk'], (1, 2, 0)).astype(BF16)
    w['w_uv'] = jnp.transpose(p['w_uv'], (1, 0, 2)).astype(BF16)
    w['w_o'] = p['w_o'][0].astype(BF16)
    w['w_router'] = p['w_router'].astype(BF16)
    w['b_gate_up'] = p['b_gate_up'][:, :, None, :]
    w['b_down'] = p['b_down'][:, :, None, :]
    return w


def kernel(x_prompt, x_sample, cache_kv_latent, cache_k_rope, state_conv, page_table, c_prompt, c_sample,
           w_ada, b_ada, g_pre_mix, g_post_mix, g_pre_ffn, g_post_ffn, w_conv_in, w_conv_dw, w_conv_out,
           w_ada_kv, b_ada_kv, g_kv_in, w_dkv, g_kv_lat, w_uk, w_uv, w_dq, g_q, w_uq, w_o,
           w_router, b_router, w_gate_up, b_gate_up, w_down, b_down):
    p = dict(g_pre_mix=g_pre_mix, g_post_mix=g_post_mix, g_pre_ffn=g_pre_ffn, g_post_ffn=g_post_ffn,
             w_conv_in=w_conv_in, w_conv_dw=w_conv_dw, w_conv_out=w_conv_out, g_kv_in=g_kv_in, w_dkv=w_dkv,
             g_kv_lat=g_kv_lat, w_uk=w_uk, w_uv=w_uv, w_dq=w_dq, g_q=g_q, w_uq=w_uq, w_o=w_o,
             w_router=w_router, b_router=b_router, w_gate_up=w_gate_up, b_gate_up=b_gate_up,
             w_down=w_down, b_down=b_down)
    w = _prepare_weights(p)
    bp, tp, d = x_prompt.shape
    bs, ts, _ = x_sample.shape
    depth = w_ada.shape[0]

    c_all = jnp.concatenate([c_prompt, c_sample], axis=0)
    mods = [_ada(c_all, w_ada[l], b_ada[l]).reshape(bp + bs, 6, d) for l in range(depth)]
    modkv = _ada(c_all, w_ada_kv, b_ada_kv).reshape(bp + bs, 2, d)

    pos_p = jnp.arange(tp, dtype=jnp.int32)
    prev_p = jnp.zeros((bp, TAIL_ROWS, d), F32)
    tiles_p = dict(bb=1, tt=min(tp, 512), bm=256, cbb=1, ctt=min(tp, 256), q_dtype=BF16)
    attend_p = lambda ql, qr, ckv, kr: _attn_prompt(ql, qr, ckv, kr, min(tp, 512))
    y_p, conv_p, lat_p, rope_p = _trunk(x_prompt, [m[:bp] for m in mods], modkv[:bp], pos_p, prev_p,
                                        attend_p, w, tiles_p)

    n_pages = page_table.shape[1]
    pos_s = n_pages * PAGE_SIZE + jnp.arange(ts, dtype=jnp.int32)
    prev_s = jnp.concatenate([jnp.zeros((bs, TAIL_ROWS - (CONV_W - 1), d), F32), state_conv[0]], axis=1)
    sbb = min(bs, 32)
    tiles_s = dict(bb=sbb, tt=ts, bm=128, cbb=sbb, ctt=ts, q_dtype=F32)
    ppc = math.gcd(n_pages, 16)
    spb = math.gcd(bs, 2)
    cache_rope_t = jnp.swapaxes(cache_k_rope, 1, 2)
    attend_s = lambda ql, qr, ckv, kr: _attn_paged(ql, qr, ckv, kr, cache_kv_latent, cache_rope_t, page_table,
                                                   ppc, spb)
    y_s, conv_s, lat_s, rope_s = _trunk(x_sample, [m[bp:] for m in mods], modkv[bp:], pos_s, prev_s,
                                        attend_s, w, tiles_s)
    return (y_p, y_s, lat_p, rope_p, conv_p, lat_s, rope_s, conv_s)
```

```python
import functools
import math

import jax
import jax.numpy as jnp
import numpy as np
from jax import lax
from jax.experimental import pallas as pl
from jax.experimental.pallas import tpu as pltpu

N_HEADS = 8
NOPE_DIM = 128
ROPE_DIM = 64
HALF_ROPE = ROPE_DIM // 2
V_DIM = 128
KV_RANK = 256
ROPE_THETA = 10000.0
SM_SCALE = (NOPE_DIM + ROPE_DIM) ** -0.5
N_EXPERTS = 32
TOP_K = 4
SWIGLU_LIMIT = 7.0
SWIGLU_ALPHA = 1.702
RMS_EPS = 1e-6
NEG_INF = -1e30
PAGE_SIZE = 128
CONV_W = 3
TAIL_ROWS = 8
MOE_BLOCK_ROWS = 256

VMEM_LIMIT = 56 * 1024 * 1024
BF16 = jnp.bfloat16
F32 = jnp.float32


def _cparams(sem):
    return pltpu.CompilerParams(dimension_semantics=sem, vmem_limit_bytes=VMEM_LIMIT)


def _const_spec(shape):
    nd = len(shape)
    return pl.BlockSpec(shape, lambda *_: (0,) * nd)


def _rms(x, g):
    ms = jnp.mean(x * x, axis=-1, keepdims=True)
    return x * lax.rsqrt(ms + RMS_EPS) * g


def _dot(a, b):
    return jnp.dot(a.astype(BF16), b.astype(BF16), preferred_element_type=F32)


def _dot_nt(a, b):
    return lax.dot_general(a.astype(BF16), b.astype(BF16), (((1,), (1,)), ((), ())),
                           preferred_element_type=F32)


def _modvec(mod_ref, j, bb, tt):
    v = mod_ref[:, j:j + 1, :]
    d = v.shape[-1]
    if bb == 1:
        return v[0]
    return jnp.broadcast_to(v, (bb, tt, d)).reshape(bb * tt, d)


def _ada_kernel(c_ref, w_ref, b_ref, o_ref):
    c = c_ref[...]
    c_act = c * jax.nn.sigmoid(c)
    o_ref[...] = _dot(c_act, w_ref[...]) + b_ref[...]


def _ada(c, w, b, tn=1024):
    m, d = c.shape
    n = w.shape[1]
    return pl.pallas_call(
        _ada_kernel,
        out_shape=jax.ShapeDtypeStruct((m, n), F32),
        grid=(n // tn,),
        in_specs=[_const_spec((m, d)), pl.BlockSpec((d, tn), lambda j: (0, j)),
                  pl.BlockSpec((1, tn), lambda j: (0, j))],
        out_specs=pl.BlockSpec((m, tn), lambda j: (0, j)),
        compiler_params=_cparams(("arbitrary",)),
        name="ada",
    )(c, w, b.reshape(1, n))


def _stage_tail(first_step, h, mix, gt_m, sh_f, sc_f, g_post, g_pre_ffn, w_router, b_router,
                cnt_ref, h_out, f_out, e_out, g_out, r_out, c_out, bb, tt):
    rows = bb * tt
    h1 = h + gt_m * _rms(mix, g_post)
    f = _rms(h1, g_pre_ffn) * (1.0 + sc_f) + sh_f
    h_out[...] = h1.reshape(bb, tt, -1)
    f_out[...] = f.reshape(bb, tt, -1)

    logits = _dot(f, w_router) + b_router
    lane = lax.broadcasted_iota(jnp.int32, logits.shape, 1)
    lane_k = lax.broadcasted_iota(jnp.int32, (rows, TOP_K), 1)
    work = logits
    multi_hot = jnp.zeros(logits.shape, F32)
    top_val, top_idx, top_sel = [], [], []
    for _ in range(TOP_K):
        m = jnp.max(work, axis=-1, keepdims=True)
        idx = jnp.min(jnp.where(work == m, lane, N_EXPERTS), axis=-1, keepdims=True)
        sel = lane == idx
        multi_hot = multi_hot + sel.astype(F32)
        work = jnp.where(sel, -jnp.inf, work)
        top_val.append(m)
        top_idx.append(idx)
        top_sel.append(sel)
    exps = [jnp.exp(v - top_val[0]) for v in top_val]
    denom = exps[0] + exps[1] + exps[2] + exps[3]

    @pl.when(first_step)
    def _():
        cnt_ref[...] = jnp.zeros_like(cnt_ref)

    ri = lax.broadcasted_iota(jnp.int32, (rows, rows), 0)
    ci = lax.broadcasted_iota(jnp.int32, (rows, rows), 1)
    strict_lower = (ri > ci).astype(BF16)
    before = jnp.dot(strict_lower, multi_hot.astype(BF16), preferred_element_type=F32) + cnt_ref[...]

    e_val = jnp.zeros((rows, TOP_K), jnp.int32)
    g_val = jnp.zeros((rows, TOP_K), F32)
    r_val = jnp.zeros((rows, TOP_K), F32)
    for k in range(TOP_K):
        rank_k = jnp.sum(jnp.where(top_sel[k], before, 0.0), axis=-1, keepdims=True)
        e_val = jnp.where(lane_k == k, top_idx[k], e_val)
        g_val = jnp.where(lane_k == k, exps[k] / denom, g_val)
        r_val = jnp.where(lane_k == k, rank_k, r_val)
    e_out[...] = e_val.reshape(bb, tt, TOP_K)
    g_out[...] = g_val.reshape(bb, tt, TOP_K)
    r_out[...] = r_val.astype(jnp.int32).reshape(bb, tt, TOP_K)
    cnt_ref[...] = cnt_ref[...] + jnp.sum(multi_hot, axis=0, keepdims=True)
    c_out[...] = cnt_ref[...]


def _tail_out_shapes(b, t, d):
    return [jax.ShapeDtypeStruct((b, t, d), F32), jax.ShapeDtypeStruct((b, t, d), F32),
            jax.ShapeDtypeStruct((b, t, TOP_K), jnp.int32), jax.ShapeDtypeStruct((b, t, TOP_K), F32),
            jax.ShapeDtypeStruct((b, t, TOP_K), jnp.int32), jax.ShapeDtypeStruct((1, N_EXPERTS), F32)]


def _tail_out_specs(bb, tt, d):
    tok = lambda n: pl.BlockSpec((bb, tt, n), lambda i, j: (i, j, 0))
    return [tok(d), tok(d), tok(TOP_K), tok(TOP_K), tok(TOP_K),
            pl.BlockSpec((1, N_EXPERTS), lambda i, j: (0, 0))]


def _conv_stage_kernel(x_ref, mod_ref, prev_ref, g_pre, w_in, w_dw, w_out, g_post, g_pre_ffn,
                       w_router, b_router,
                       h_out, f_out, e_out, g_out, r_out, c_out, u_tail,
                       carry, cnt_ref, *, bb, tt, d):
    rows = bb * tt
    i, j = pl.program_id(0), pl.program_id(1)
    x = x_ref[...].reshape(rows, d)
    mv = lambda k: _modvec(mod_ref, k, bb, tt)
    a = _rms(x, g_pre[...]) * (1.0 + mv(1)) + mv(0)
    bcv = _dot(a, w_in[...])
    b_gate, c_gate, v = bcv[:, :d], bcv[:, d:2 * d], bcv[:, 2 * d:]
    u = c_gate * v

    tpos = lax.broadcasted_iota(jnp.int32, (rows, 1), 0) % tt
    if bb == 1:
        @pl.when(j == 0)
        def _():
            carry[...] = prev_ref[0]
        p0, p1 = carry[TAIL_ROWS - 2:TAIL_ROWS - 1, :], carry[TAIL_ROWS - 1:TAIL_ROWS, :]
        u_m1 = jnp.where(tpos == 0, p1, pltpu.roll(u, 1, 0))
        u_m2 = jnp.where(tpos == 0, p0, jnp.where(tpos == 1, p1, pltpu.roll(u, 2, 0)))
        carry[...] = u[rows - TAIL_ROWS:, :]
        u_tail[0] = u[rows - TAIL_ROWS:, :]
    else:
        pp = prev_ref[...].reshape(rows, d)
        u_m1 = jnp.where(tpos == 0, pltpu.roll(pp, rows - (TAIL_ROWS - 1), 0), pltpu.roll(u, 1, 0))
        u_m2 = jnp.where(tpos < 2, pltpu.roll(pp, rows - (TAIL_ROWS - 2), 0), pltpu.roll(u, 2, 0))
        u_tail[...] = u.reshape(bb, tt, d)
    conv = w_dw[0:1, :] * u_m2 + w_dw[1:2, :] * u_m1 + w_dw[2:3, :] * u
    mix = _dot(b_gate * conv, w_out[...])

    _stage_tail(jnp.logical_and(i == 0, j == 0), x, mix, mv(2), mv(3), mv(4), g_post[...], g_pre_ffn[...],
                w_router[...], b_router[...], cnt_ref, h_out, f_out, e_out, g_out, r_out, c_out, bb, tt)


def _conv_stage(x, mod, prev_pad, g_pre, w_in, w_dw, w_out, g_post, g_pre_ffn, w_router, b_router, bb, tt):
    b, t, d = x.shape
    assert b % bb == 0 and t % tt == 0 and (bb == 1 or tt == t == TAIL_ROWS)
    tok = lambda n: pl.BlockSpec((bb, tt, n), lambda i, j: (i, j, 0))
    seq = lambda n: pl.BlockSpec((bb, n, d), lambda i, j: (i, 0, 0))
    kern = functools.partial(_conv_stage_kernel, bb=bb, tt=tt, d=d)
    outs = pl.pallas_call(
        kern,
        out_shape=_tail_out_shapes(b, t, d) + [jax.ShapeDtypeStruct((b, TAIL_ROWS, d), F32)],
        grid=(b // bb, t // tt),
        in_specs=[tok(d), seq(6), seq(TAIL_ROWS), _const_spec((1, d)), _const_spec((d, 3 * d)),
                  _const_spec((CONV_W, d)), _const_spec((d, d)), _const_spec((1, d)), _const_spec((1, d)),
                  _const_spec((d, N_EXPERTS)), _const_spec((1, N_EXPERTS))],
        out_specs=_tail_out_specs(bb, tt, d) + [seq(TAIL_ROWS)],
        scratch_shapes=[pltpu.VMEM((TAIL_ROWS, d), F32), pltpu.VMEM((1, N_EXPERTS), F32)],
        compiler_params=_cparams(("arbitrary", "arbitrary")),
        name="conv_stage",
    )(x, mod, prev_pad, g_pre, w_in, w_dw, w_out, g_post, g_pre_ffn, w_router, b_router)
    return outs


RING = 3
LEAD_BLOCKS = RING


def _moe_kernel(asg_ref, be_ref, x_hbm, wgu, bgu, wd, bd, y_hbm, xbuf0, xbuf1, xbuf2, obuf0, obuf1, obuf2,
                wgu_b, wd_b, sem_in, sem_out, *, bm, d_ff):
    i = pl.program_id(0)
    last = pl.num_programs(0) - 1
    xbufs, obufs = (xbuf0, xbuf1, xbuf2), (obuf0, obuf1, obuf2)

    def gather(blk, s):
        for r in range(bm):
            tok = jnp.right_shift(asg_ref[blk + LEAD_BLOCKS, r], TOP_K.bit_length() - 1)
            pltpu.make_async_copy(x_hbm.at[pl.ds(tok, 1)], xbufs[s].at[pl.ds(r, 1)], sem_in.at[s]).start()

    def scatter(blk, s):
        for r in range(bm):
            pltpu.make_async_copy(obufs[s].at[pl.ds(r, 1)], y_hbm.at[pl.ds(asg_ref[blk + LEAD_BLOCKS, r], 1)],
                                  sem_out.at[s]).start()

    def wait_gather(s):
        pltpu.make_async_copy(x_hbm.at[pl.ds(0, bm)], xbufs[s], sem_in.at[s]).wait()

    def wait_scatter(s):
        pltpu.make_async_copy(obufs[s], y_hbm.at[pl.ds(0, bm)], sem_out.at[s]).wait()

    @pl.when(i == 0)
    def _():
        for s in range(RING):
            obufs[s][...] = jnp.zeros_like(obufs[s])
        gather(0, 0)
        gather(jnp.minimum(1, last), 1)
        scatter(-3, 0)
        scatter(-2, 1)

    @pl.when(jnp.logical_or(i == 0, be_ref[i] != be_ref[jnp.maximum(i - 1, 0)]))
    def _():
        wgu_b[...] = wgu[0, 0].astype(BF16)
        wd_b[...] = wd[0, 0].astype(BF16)

    def body(s):
        prev = (s + RING - 1) % RING
        wait_gather(s)
        wait_scatter(s)
        gather(jnp.minimum(i + 2, last), prev)
        scatter(i - 1, prev)
        hgu = _dot(xbufs[s][...], wgu_b[...]) + bgu[0, 0]
        gate = jnp.minimum(hgu[:, :d_ff], SWIGLU_LIMIT)
        up = jnp.clip(hgu[:, d_ff:], -SWIGLU_LIMIT, SWIGLU_LIMIT)
        glu = gate * jax.nn.sigmoid(gate * SWIGLU_ALPHA)
        obufs[s][...] = _dot((up + 1.0) * glu, wd_b[...]) + bd[0, 0]

        @pl.when(i == last)
        def _():
            scatter(i, s)
            for q in range(RING):
                wait_scatter(q)
            wait_gather((s + 1) % RING)
            wait_gather(prev)

    for s in range(RING):
        pl.when(i % RING == s)(functools.partial(body, s))


def _moe(x2d, asg, block_e, n_out, layer, wgu, bgu, wd, bd, bm):
    n, d = x2d.shape
    n_blocks = asg.shape[0] - LEAD_BLOCKS
    d_ff = wd.shape[2]
    kern = functools.partial(_moe_kernel, bm=bm, d_ff=d_ff)
    grid_spec = pltpu.PrefetchScalarGridSpec(
        num_scalar_prefetch=2,
        grid=(n_blocks,),
        in_specs=[pl.BlockSpec(memory_space=pl.ANY),
                  pl.BlockSpec((1, 1, d, 2 * d_ff), lambda i, t, e: (layer, e[i], 0, 0)),
                  pl.BlockSpec((1, 1, 1, 2 * d_ff), lambda i, t, e: (layer, e[i], 0, 0)),
                  pl.BlockSpec((1, 1, d_ff, d), lambda i, t, e: (layer, e[i], 0, 0)),
                  pl.BlockSpec((1, 1, 1, d), lambda i, t, e: (layer, e[i], 0, 0))],
        out_specs=pl.BlockSpec(memory_space=pl.ANY),
        scratch_shapes=[pltpu.VMEM((bm, d), F32)] * (2 * RING)
                       + [pltpu.VMEM((d, 2 * d_ff), BF16), pltpu.VMEM((d_ff, d), BF16),
                          pltpu.SemaphoreType.DMA((RING,)), pltpu.SemaphoreType.DMA((RING,))],
    )
    return pl.pallas_call(
        kern,
        out_shape=jax.ShapeDtypeStruct((n_out, d), F32),
        grid_spec=grid_spec,
        compiler_params=_cparams(("arbitrary",)),
        name="moe",
    )(asg, block_e, x2d, wgu, bgu, wd, bd)


def _combine_kernel(y_ref, gates_ref, h_ref, mod_ref, g_post, o_ref, *, bb, tt, d):
    rows = bb * tt
    g = gates_ref[...].reshape(rows, TOP_K)
    f = g[:, 0:1] * y_ref[:, 0:d]
    for k in range(1, TOP_K):
        f = f + g[:, k:k + 1] * y_ref[:, k * d:(k + 1) * d]
    h = h_ref[...].reshape(rows, d)
    gt_f = _modvec(mod_ref, 5, bb, tt)
    o_ref[...] = (h + gt_f * _rms(f, g_post[...])).reshape(bb, tt, d)


def _combine(y_tok, tok0, gates, h1, mod, g_post, bb, tt):
    b, t, d = h1.shape
    n_j = t // tt
    rows = bb * tt
    assert tok0 % rows == 0
    blk0 = tok0 // rows
    tok = lambda n: pl.BlockSpec((bb, tt, n), lambda i, j: (i, j, 0))
    kern = functools.partial(_combine_kernel, bb=bb, tt=tt, d=d)
    return pl.pallas_call(
        kern,
        out_shape=jax.ShapeDtypeStruct((b, t, d), F32),
        grid=(b // bb, n_j),
        in_specs=[pl.BlockSpec((rows, TOP_K * d), lambda i, j: (blk0 + i * n_j + j, 0)), tok(TOP_K), tok(d),
                  pl.BlockSpec((bb, 6, d), lambda i, j: (i, 0, 0)), _const_spec((1, d))],
        out_specs=tok(d),
        compiler_params=_cparams(("arbitrary", "arbitrary")),
        name="combine",
    )(y_tok, gates, h1, mod, g_post)


def _mla_pre_kernel(h_ref, mod_ref, modkv_ref, g_pre, g_kv_in, w_dkv, g_kv_lat, ck_ref, sk_ref,
                    w_dq, g_q, w_uq, w_ukt, cq_ref, sq_ref,
                    ckv_out, kr_out, ql_out, qr_out, *, bb, tt, d):
    rows = bb * tt
    h = h_ref[...].reshape(rows, d)
    ms = jnp.mean(h * h, axis=-1, keepdims=True)
    hn = h * lax.rsqrt(ms + RMS_EPS)

    kv_in = (hn * g_kv_in[...]) * (1.0 + _modvec(modkv_ref, 1, bb, tt)) + _modvec(modkv_ref, 0, bb, tt)
    lat = _dot(kv_in, w_dkv[...])
    ckv_out[...] = _rms(lat[:, :KV_RANK], g_kv_lat[...]).reshape(bb, tt, KV_RANK)
    kx = lat[:, KV_RANK:KV_RANK + ROPE_DIM].reshape(bb, tt, ROPE_DIM)
    kxs = lat[:, KV_RANK + ROPE_DIM:].reshape(bb, tt, ROPE_DIM)
    kr_out[...] = kx * ck_ref[...][None] + kxs * sk_ref[...][None]

    a = (hn * g_pre[...]) * (1.0 + _modvec(mod_ref, 1, bb, tt)) + _modvec(mod_ref, 0, bb, tt)
    qc = _rms(_dot(a, w_dq[...]), g_q[...])
    q = _dot(qc, w_uq[...])
    n_nope, n_rope = N_HEADS * NOPE_DIM, N_HEADS * ROPE_DIM
    qx = q[:, n_nope:n_nope + n_rope].reshape(bb, tt, n_rope)
    qxs = q[:, n_nope + n_rope:].reshape(bb, tt, n_rope)
    q_rope = qx * cq_ref[...][None] + qxs * sq_ref[...][None]
    for hd in range(N_HEADS):
        q_lat = _dot(q[:, hd * NOPE_DIM:(hd + 1) * NOPE_DIM], w_ukt[hd])
        ql_out[:, hd] = q_lat.reshape(bb, tt, KV_RANK).astype(ql_out.dtype)
        qr_out[:, hd] = q_rope[:, :, hd * ROPE_DIM:(hd + 1) * ROPE_DIM].astype(qr_out.dtype)


def _mla_pre(h, mod, modkv, g_pre, g_kv_in, w_dkv, g_kv_lat, ck, sk, w_dq, g_q, w_uq, w_ukt, cq, sq, bb, tt,
             q_dtype):
    b, t, d = h.shape
    q_rank = w_dq.shape[1]
    tok = lambda n: pl.BlockSpec((bb, tt, n), lambda i, j: (i, j, 0))
    head = lambda n: pl.BlockSpec((bb, N_HEADS, tt, n), lambda i, j: (i, 0, j, 0))
    seq = lambda n: pl.BlockSpec((bb, n, d), lambda i, j: (i, 0, 0))
    pos = lambda n: pl.BlockSpec((tt, n), lambda i, j: (j, 0))
    kern = functools.partial(_mla_pre_kernel, bb=bb, tt=tt, d=d)
    return pl.pallas_call(
        kern,
        out_shape=[jax.ShapeDtypeStruct((b, t, KV_RANK), F32), jax.ShapeDtypeStruct((b, t, ROPE_DIM), F32),
                   jax.ShapeDtypeStruct((b, N_HEADS, t, KV_RANK), q_dtype),
                   jax.ShapeDtypeStruct((b, N_HEADS, t, ROPE_DIM), q_dtype)],
        grid=(b // bb, t // tt),
        in_specs=[tok(d), seq(6), seq(2), _const_spec((1, d)), _const_spec((1, d)),
                  _const_spec(w_dkv.shape), _const_spec((1, KV_RANK)), pos(ROPE_DIM), pos(ROPE_DIM),
                  _const_spec(w_dq.shape), _const_spec((1, q_rank)), _const_spec(w_uq.shape),
                  _const_spec(w_ukt.shape), pos(N_HEADS * ROPE_DIM), pos(N_HEADS * ROPE_DIM)],
        out_specs=[tok(KV_RANK), tok(ROPE_DIM), head(KV_RANK), head(ROPE_DIM)],
        compiler_params=_cparams(("arbitrary", "arbitrary")),
        name="mla_pre",
    )(h, mod, modkv, g_pre, g_kv_in, w_dkv, g_kv_lat, ck, sk, w_dq, g_q, w_uq, w_ukt, cq, sq)


LANES = 128


def _softmax_step(s, m_prev, l_prev):
    keys = s.shape[1]
    m_new = jnp.maximum(m_prev, jnp.max(s, axis=-1, keepdims=True))
    alpha = jnp.exp(m_prev - m_new)
    p = jnp.exp(s - (jnp.tile(m_new, (1, keys // LANES)) if keys >= LANES else m_new[:, :keys]))
    l_new = alpha * l_prev + jnp.sum(p, axis=-1, keepdims=True)
    return p, m_new, l_new, alpha


def _attn_prompt_kernel(ql_ref, qr_ref, ckv_ref, kr_ref, o_ref, m_ref, l_ref, acc_ref, *, tq):
    i, j = pl.program_id(1), pl.program_id(2)
    rows = N_HEADS * tq

    @pl.when(j == 0)
    def _():
        m_ref[...] = jnp.full_like(m_ref, -jnp.inf)
        l_ref[...] = jnp.zeros_like(l_ref)
        acc_ref[...] = jnp.zeros_like(acc_ref)

    def step(masked):
        ckv = ckv_ref[0].astype(BF16)
        kr = kr_ref[0].astype(BF16)
        q = ql_ref[0].reshape(rows, KV_RANK)
        qr = qr_ref[0].reshape(rows, ROPE_DIM)
        s = (_dot_nt(q, ckv) + _dot_nt(qr, kr)) * SM_SCALE
        if masked:
            q_pos = lax.broadcasted_iota(jnp.int32, (rows, tq), 0) % tq
            k_pos = lax.broadcasted_iota(jnp.int32, (rows, tq), 1)
            s = jnp.where(k_pos <= q_pos, s, NEG_INF)
        p, m_new, l_new, alpha = _softmax_step(s, m_ref[...], l_ref[...])
        m_ref[...] = m_new
        l_ref[...] = l_new
        acc_ref[...] = (jnp.tile(alpha, (1, KV_RANK // LANES)) * acc_ref[...]
                        + jnp.dot(p.astype(BF16), ckv, preferred_element_type=F32))

    @pl.when(j < i)
    def _():
        step(False)

    @pl.when(j == i)
    def _():
        step(True)
        out = acc_ref[...] / jnp.tile(l_ref[...], (1, KV_RANK // LANES))
        o_ref[0] = out.reshape(N_HEADS, tq, KV_RANK).astype(o_ref.dtype)


def _attn_prompt(q_lat, q_rope, c_kv, k_rope, tq):
    b, _, t, _ = q_lat.shape
    n = t // tq
    rows = N_HEADS * tq
    qspec = lambda w: pl.BlockSpec((1, N_HEADS, tq, w), lambda bi, i, j: (bi, 0, i, 0))
    kspec = lambda w: pl.BlockSpec((1, tq, w), lambda bi, i, j: (bi, jnp.minimum(i, j), 0))
    return pl.pallas_call(
        functools.partial(_attn_prompt_kernel, tq=tq),
        out_shape=jax.ShapeDtypeStruct((b, N_HEADS, t, KV_RANK), BF16),
        grid=(b, n, n),
        in_specs=[qspec(KV_RANK), qspec(ROPE_DIM), kspec(KV_RANK), kspec(ROPE_DIM)],
        out_specs=qspec(KV_RANK),
        scratch_shapes=[pltpu.VMEM((rows, LANES), F32), pltpu.VMEM((rows, LANES), F32),
                        pltpu.VMEM((rows, KV_RANK), F32)],
        compiler_params=_cparams(("arbitrary", "arbitrary", "arbitrary")),
        name="attn_prompt",
    )(q_lat, q_rope, c_kv, k_rope)


def _attn_paged_kernel(pt_ref, ql_ref, qr_ref, cnew_ref, krnew_ref, *rest, ppc, spb, t_new):
    lat_pages = rest[:spb * ppc]
    rope_pages = rest[spb * ppc:2 * spb * ppc]
    o_ref = rest[2 * spb * ppc]
    kbuf, m_ref, l_ref, acc_ref = rest[2 * spb * ppc + 1:]
    j = pl.program_id(1)
    rows = N_HEADS * t_new

    @pl.when(j == 0)
    def _():
        m_ref[...] = jnp.full_like(m_ref, -jnp.inf)
        l_ref[...] = jnp.zeros_like(l_ref)
        acc_ref[...] = jnp.zeros_like(acc_ref)

    for sq in range(spb):
        q = ql_ref[sq].reshape(rows, KV_RANK).astype(BF16)
        qr = qr_ref[sq].reshape(rows, ROPE_DIM).astype(BF16)
        s_pages = []
        for k in range(ppc):
            kb = lat_pages[sq * ppc + k][0].astype(BF16)
            kbuf[sq, k * PAGE_SIZE:(k + 1) * PAGE_SIZE, :] = kb
            s_pages.append(_dot_nt(q, kb) + _dot(qr, rope_pages[sq * ppc + k][0]))
        s = jnp.concatenate(s_pages, axis=1) * SM_SCALE
        p, m_new, l_new, alpha = _softmax_step(s, m_ref[sq], l_ref[sq])
        m_ref[sq] = m_new
        l_ref[sq] = l_new
        acc_ref[sq] = (jnp.tile(alpha, (1, KV_RANK // LANES)) * acc_ref[sq]
                       + jnp.dot(p.astype(BF16), kbuf[sq], preferred_element_type=F32))

    @pl.when(j == pl.num_programs(1) - 1)
    def _():
        for sq in range(spb):
            q = ql_ref[sq].reshape(rows, KV_RANK).astype(BF16)
            qr = qr_ref[sq].reshape(rows, ROPE_DIM).astype(BF16)
            c_new = cnew_ref[sq].astype(BF16)
            s_new = (_dot_nt(q, c_new) + _dot_nt(qr, krnew_ref[sq])) * SM_SCALE
            q_pos = lax.broadcasted_iota(jnp.int32, (rows, t_new), 0) % t_new
            k_pos = lax.broadcasted_iota(jnp.int32, (rows, t_new), 1)
            s_new = jnp.where(k_pos <= q_pos, s_new, NEG_INF)
            p, m_new, l_new, alpha = _softmax_step(s_new, m_ref[sq], l_ref[sq])
            acc = (jnp.tile(alpha, (1, KV_RANK // LANES)) * acc_ref[sq]
                   + jnp.dot(p.astype(BF16), c_new, preferred_element_type=F32))
            out = acc / jnp.tile(l_new, (1, KV_RANK // LANES))
            o_ref[sq] = out.reshape(N_HEADS, t_new, KV_RANK)


def _attn_paged(q_lat, q_rope, c_new, kr_new, cache_lat, cache_rope_t, page_table, ppc, spb):
    s, _, t_new, _ = q_lat.shape
    n_pages = page_table.shape[1]
    assert n_pages % ppc == 0 and s % spb == 0
    rows = N_HEADS * t_new
    head = lambda w: pl.BlockSpec((spb, N_HEADS, t_new, w), lambda b, j, pt: (b, 0, 0, 0))
    seq = lambda w: pl.BlockSpec((spb, t_new, w), lambda b, j, pt: (b, 0, 0))
    lat_page = lambda sq, k: pl.BlockSpec((1, PAGE_SIZE, KV_RANK),
                                          lambda b, j, pt: (pt[b * spb + sq, j * ppc + k], 0, 0))
    rope_page = lambda sq, k: pl.BlockSpec((1, ROPE_DIM, PAGE_SIZE),
                                           lambda b, j, pt: (pt[b * spb + sq, j * ppc + k], 0, 0))
    pairs = [(sq, k) for sq in range(spb) for k in range(ppc)]
    grid_spec = pltpu.PrefetchScalarGridSpec(
        num_scalar_prefetch=1,
        grid=(s // spb, n_pages // ppc),
        in_specs=[head(KV_RANK), head(ROPE_DIM), seq(KV_RANK), seq(ROPE_DIM)]
                 + [lat_page(sq, k) for sq, k in pairs] + [rope_page(sq, k) for sq, k in pairs],
        out_specs=head(KV_RANK),
        scratch_shapes=[pltpu.VMEM((spb, ppc * PAGE_SIZE, KV_RANK), BF16),
                        pltpu.VMEM((spb, rows, LANES), F32), pltpu.VMEM((spb, rows, LANES), F32),
                        pltpu.VMEM((spb, rows, KV_RANK), F32)],
    )
    n_in = spb * ppc
    return pl.pallas_call(
        functools.partial(_attn_paged_kernel, ppc=ppc, spb=spb, t_new=t_new),
        out_shape=jax.ShapeDtypeStruct((s, N_HEADS, t_new, KV_RANK), F32),
        grid_spec=grid_spec,
        compiler_params=_cparams(("arbitrary", "arbitrary")),
        name="attn_paged",
    )(page_table, q_lat, q_rope, c_new, kr_new, *([cache_lat] * n_in), *([cache_rope_t] * n_in))


def _mla_post_kernel(h_ref, ol_ref, mod_ref, w_uv, w_o, g_post, g_pre_ffn, w_router, b_router,
                     h_out, f_out, e_out, g_out, r_out, c_out, obuf, cnt_ref, *, bb, tt, d):
    rows = bb * tt
    i, j = pl.program_id(0), pl.program_id(1)
    h = h_ref[...].reshape(rows, d)
    for hd in range(N_HEADS):
        o_lat = ol_ref[:, hd].reshape(rows, KV_RANK)
        obuf[:, hd * V_DIM:(hd + 1) * V_DIM] = _dot(o_lat, w_uv[hd]).astype(BF16)
    mix = jnp.dot(obuf[...], w_o[...], preferred_element_type=F32)
    mv = lambda k: _modvec(mod_ref, k, bb, tt)
    _stage_tail(jnp.logical_and(i == 0, j == 0), h, mix, mv(2), mv(3), mv(4), g_post[...], g_pre_ffn[...],
                w_router[...], b_router[...], cnt_ref, h_out, f_out, e_out, g_out, r_out, c_out, bb, tt)


def _mla_post(h, o_lat, mod, w_uv, w_o, g_post, g_pre_ffn, w_router, b_router, bb, tt):
    b, t, d = h.shape
    tok = lambda n: pl.BlockSpec((bb, tt, n), lambda i, j: (i, j, 0))
    kern = functools.partial(_mla_post_kernel, bb=bb, tt=tt, d=d)
    return pl.pallas_call(
        kern,
        out_shape=_tail_out_shapes(b, t, d),
        grid=(b // bb, t // tt),
        in_specs=[tok(d), pl.BlockSpec((bb, N_HEADS, tt, KV_RANK), lambda i, j: (i, 0, j, 0)),
                  pl.BlockSpec((bb, 6, d), lambda i, j: (i, 0, 0)),
                  _const_spec(w_uv.shape), _const_spec(w_o.shape), _const_spec((1, d)), _const_spec((1, d)),
                  _const_spec((d, N_EXPERTS)), _const_spec((1, N_EXPERTS))],
        out_specs=_tail_out_specs(bb, tt, d),
        scratch_shapes=[pltpu.VMEM((bb * tt, N_HEADS * V_DIM), BF16), pltpu.VMEM((1, N_EXPERTS), F32)],
        compiler_params=_cparams(("arbitrary", "arbitrary")),
        name="mla_post",
    )(h, o_lat, mod, w_uv, w_o, g_post, g_pre_ffn, w_router, b_router)


def _moe_layer(f, top_e, gates, rank, counts, h1, mod, g_post, layer, wgu, bgu, wd, bd, bm, bb, tt):
    d = f[0].shape[-1]
    n_tok = [x.shape[0] * x.shape[1] for x in f]
    n_assign = sum(n_tok) * TOP_K
    n_blocks = (n_assign + N_EXPERTS * (bm - 1) + bm - 1) // bm
    counts = [c.reshape(N_EXPERTS).astype(jnp.int32) for c in counts]
    padded = (sum(counts) + bm - 1) // bm * bm
    pad_end = jnp.cumsum(padded)
    base = pad_end - padded
    dests, starts, tok0 = [], [], 0
    for g, n in enumerate(n_tok):
        flat_e = top_e[g].reshape(n * TOP_K)
        dests.append((base[flat_e] + rank[g].reshape(n * TOP_K)).astype(jnp.int32))
        base = base + counts[g]
        starts.append(tok0)
        tok0 += n
    slot = jnp.arange((n_blocks + LEAD_BLOCKS) * bm, dtype=jnp.int32)
    dummy = n_assign + (slot // bm % RING) * bm + slot % bm
    asg = dummy.at[jnp.concatenate(dests) + LEAD_BLOCKS * bm].set(
        jnp.arange(n_assign, dtype=jnp.int32), unique_indices=True, mode='promise_in_bounds')
    block_start = jnp.arange(n_blocks, dtype=jnp.int32) * bm
    block_e = jnp.minimum(jnp.sum(pad_end[None, :] <= block_start[:, None], axis=1), N_EXPERTS - 1).astype(jnp.int32)
    n_dummy_tok = RING * bm // TOP_K
    x2d = jnp.concatenate([x.reshape(n, d) for x, n in zip(f, n_tok)] + [jnp.zeros((n_dummy_tok, d), F32)], axis=0)
    y = _moe(x2d, asg.reshape(n_blocks + LEAD_BLOCKS, bm), block_e, n_assign + RING * bm, layer, wgu, bgu, wd, bd, bm)
    y_tok = y.reshape(tok0 + n_dummy_tok, TOP_K * d)
    return [_combine(y_tok, starts[g], gates[g], h1[g], mod[g], g_post, bb[g], tt[g]) for g in range(len(f))]


def _rope_tables(pos):
    inv = jnp.exp(-jnp.log(ROPE_THETA) * jnp.arange(HALF_ROPE, dtype=jnp.float32) / HALF_ROPE)
    ang = pos.astype(jnp.float32)[:, None] * inv[None, :]
    cos, sin = jnp.cos(ang), jnp.sin(ang)
    ck = jnp.concatenate([cos, cos], axis=-1)
    sk = jnp.concatenate([-sin, sin], axis=-1)
    return ck, sk, jnp.tile(ck, (1, N_HEADS)), jnp.tile(sk, (1, N_HEADS))


def _trunk(groups, w, bm):
    row = lambda v: v.reshape(1, -1)
    col = lambda outs, k: [o[k] for o in outs]

    def moe(stage, layer):
        return _moe_layer(col(stage, 1), col(stage, 2), col(stage, 3), col(stage, 4), col(stage, 5), col(stage, 0),
                          [g['mods'][layer] for g in groups], row(w['g_post_ffn'][layer]), layer,
                          w['w_gate_up'], w['b_gate_up'], w['w_down'], w['b_down'], bm,
                          [g['tiles']['cbb'] for g in groups], [g['tiles']['ctt'] for g in groups])

    stage = [_conv_stage(
        g['x'], g['mods'][0], g['prev'], row(w['g_pre_mix'][0]), w['w_conv_in'], w['w_conv_dw'], w['w_conv_out'],
        row(w['g_post_mix'][0]), row(w['g_pre_ffn'][0]), w['w_router'][0], row(w['b_router'][0]),
        g['tiles']['bb'], g['tiles']['tt']) for g in groups]
    u_tails = col(stage, 6)
    hs = moe(stage, 0)

    stage, lat, rope = [], [], []
    for g, h in zip(groups, hs):
        t = g['tiles']
        ck, sk, cq, sq = _rope_tables(g['pos'])
        c_kv, k_rope, q_lat, q_rope = _mla_pre(
            h, g['mods'][1], g['modkv'], row(w['g_pre_mix'][1]), row(w['g_kv_in']), w['w_dkv'], row(w['g_kv_lat']),
            ck, sk, w['w_dq'], row(w['g_q']), w['w_uq'], w['w_ukt'], cq, sq, t['bb'], t['tt'], t['q_dtype'])
        o_lat = g['attend'](q_lat, q_rope, c_kv, k_rope)
        stage.append(_mla_post(h, o_lat, g['mods'][1], w['w_uv'], w['w_o'], row(w['g_post_mix'][1]),
                               row(w['g_pre_ffn'][1]), w['w_router'][1], row(w['b_router'][1]), t['bb'], t['tt']))
        lat.append(c_kv)
        rope.append(k_rope)
    ys = moe(stage, 1)
    conv = [u[None, :, TAIL_ROWS - (CONV_W - 1):, :] for u in u_tails]
    return ys, conv, lat, rope


def _prepare_weights(p):
    w = dict(p)
    w['w_conv_in'] = p['w_conv_in'][0].astype(BF16)
    w['w_conv_dw'] = p['w_conv_dw'][0]
    w['w_conv_out'] = p['w_conv_out'][0].astype(BF16)
    wk = p['w_dkv']
    w['w_dkv'] = jnp.concatenate([wk, wk[:, KV_RANK + HALF_ROPE:], wk[:, KV_RANK:KV_RANK + HALF_ROPE]],
                                 axis=1).astype(BF16)
    wq = p['w_uq'][0].reshape(-1, N_HEADS, NOPE_DIM + ROPE_DIM)
    q_rank = wq.shape[0]
    nope = wq[:, :, :NOPE_DIM].reshape(q_rank, N_HEADS * NOPE_DIM)
    rope = wq[:, :, NOPE_DIM:]
    rope_sw = jnp.concatenate([rope[:, :, HALF_ROPE:], rope[:, :, :HALF_ROPE]], axis=-1)
    w['w_uq'] = jnp.concatenate([nope, rope.reshape(q_rank, -1), rope_sw.reshape(q_rank, -1)], axis=1).astype(BF16)
    w['w_dq'] = p['w_dq'][0].astype(BF16)
    w['g_q'] = p['g_q'][0]
    w['w_ukt'] = jnp.transpose(p['w_uk'], (1, 2, 0)).astype(BF16)
    w['w_uv'] = jnp.transpose(p['w_uv'], (1, 0, 2)).astype(BF16)
    w['w_o'] = p['w_o'][0].astype(BF16)
    w['w_router'] = p['w_router'].astype(BF16)
    w['b_gate_up'] = p['b_gate_up'][:, :, None, :]
    w['b_down'] = p['b_down'][:, :, None, :]
    return w


def kernel(x_prompt, x_sample, cache_kv_latent, cache_k_rope, state_conv, page_table, c_prompt, c_sample,
           w_ada, b_ada, g_pre_mix, g_post_mix, g_pre_ffn, g_post_ffn, w_conv_in, w_conv_dw, w_conv_out,
           w_ada_kv, b_ada_kv, g_kv_in, w_dkv, g_kv_lat, w_uk, w_uv, w_dq, g_q, w_uq, w_o,
           w_router, b_router, w_gate_up, b_gate_up, w_down, b_down):
    p = dict(g_pre_mix=g_pre_mix, g_post_mix=g_post_mix, g_pre_ffn=g_pre_ffn, g_post_ffn=g_post_ffn,
             w_conv_in=w_conv_in, w_conv_dw=w_conv_dw, w_conv_out=w_conv_out, g_kv_in=g_kv_in, w_dkv=w_dkv,
             g_kv_lat=g_kv_lat, w_uk=w_uk, w_uv=w_uv, w_dq=w_dq, g_q=g_q, w_uq=w_uq, w_o=w_o,
             w_router=w_router, b_router=b_router, w_gate_up=w_gate_up, b_gate_up=b_gate_up,
             w_down=w_down, b_down=b_down)
    w = _prepare_weights(p)
    bp, tp, d = x_prompt.shape
    bs, ts, _ = x_sample.shape
    depth = w_ada.shape[0]

    c_all = jnp.concatenate([c_prompt, c_sample], axis=0)
    mods = [_ada(c_all, w_ada[l], b_ada[l]).reshape(bp + bs, 6, d) for l in range(depth)]
    modkv = _ada(c_all, w_ada_kv, b_ada_kv).reshape(bp + bs, 2, d)

    pos_p = jnp.arange(tp, dtype=jnp.int32)
    prev_p = jnp.zeros((bp, TAIL_ROWS, d), F32)
    tiles_p = dict(bb=1, tt=min(tp, 512), cbb=1, ctt=min(tp, 256), q_dtype=BF16)
    attend_p = lambda ql, qr, ckv, kr: _attn_prompt(ql, qr, ckv, kr, min(tp, 512))
    prompt = dict(x=x_prompt, mods=[m[:bp] for m in mods], modkv=modkv[:bp], pos=pos_p, prev=prev_p,
                  attend=attend_p, tiles=tiles_p)

    n_pages = page_table.shape[1]
    pos_s = n_pages * PAGE_SIZE + jnp.arange(ts, dtype=jnp.int32)
    prev_s = jnp.concatenate([jnp.zeros((bs, TAIL_ROWS - (CONV_W - 1), d), F32), state_conv[0]], axis=1)
    sbb = min(bs, 32)
    tiles_s = dict(bb=sbb, tt=ts, cbb=sbb, ctt=ts, q_dtype=F32)
    ppc = math.gcd(n_pages, 16)
    spb = math.gcd(bs, 2)
    cache_rope_t = jnp.swapaxes(cache_k_rope, 1, 2)
    attend_s = lambda ql, qr, ckv, kr: _attn_paged(ql, qr, ckv, kr, cache_kv_latent, cache_rope_t, page_table,
                                                   ppc, spb)
    sample = dict(x=x_sample, mods=[m[bp:] for m in mods], modkv=modkv[bp:], pos=pos_s, prev=prev_s,
                  attend=attend_s, tiles=tiles_s)

    (y_p, y_s), (conv_p, conv_s), (lat_p, lat_s), (rope_p, rope_s) = _trunk([prompt, sample], w, MOE_BLOCK_ROWS)
    return (y_p, y_s, lat_p, rope_p, conv_p, lat_s, rope_s, conv_s)
```

```python
import functools
import math

import jax
import jax.numpy as jnp
import numpy as np
from jax import lax
from jax.experimental import pallas as pl
from jax.experimental.pallas import tpu as pltpu

N_HEADS = 8
NOPE_DIM = 128
ROPE_DIM = 64
HALF_ROPE = ROPE_DIM // 2
V_DIM = 128
KV_RANK = 256
ROPE_THETA = 10000.0
SM_SCALE = (NOPE_DIM + ROPE_DIM) ** -0.5
N_EXPERTS = 32
TOP_K = 4
SWIGLU_LIMIT = 7.0
SWIGLU_ALPHA = 1.702
RMS_EPS = 1e-6
NEG_INF = -1e30
PAGE_SIZE = 128
CONV_W = 3
TAIL_ROWS = 8
MOE_BLOCK_ROWS = 256

VMEM_LIMIT = 56 * 1024 * 1024
BF16 = jnp.bfloat16
F32 = jnp.float32


def _cparams(sem):
    return pltpu.CompilerParams(dimension_semantics=sem, vmem_limit_bytes=VMEM_LIMIT)


def _const_spec(shape):
    nd = len(shape)
    return pl.BlockSpec(shape, lambda *_: (0,) * nd)


def _rms(x, g):
    ms = jnp.mean(x * x, axis=-1, keepdims=True)
    return x * lax.rsqrt(ms + RMS_EPS) * g


def _dot(a, b):
    return jnp.dot(a.astype(BF16), b.astype(BF16), preferred_element_type=F32)


def _dot_nt(a, b):
    return lax.dot_general(a.astype(BF16), b.astype(BF16), (((1,), (1,)), ((), ())),
                           preferred_element_type=F32)


def _modvec(mod_ref, j, bb, tt):
    v = mod_ref[:, j:j + 1, :]
    d = v.shape[-1]
    if bb == 1:
        return v[0]
    return jnp.broadcast_to(v, (bb, tt, d)).reshape(bb * tt, d)


def _ada_kernel(c_ref, w_ref, b_ref, o_ref):
    c = c_ref[...]
    c_act = c * jax.nn.sigmoid(c)
    o_ref[...] = _dot(c_act, w_ref[...]) + b_ref[...]


def _ada(c, w, b, tn=1024):
    m, d = c.shape
    n = w.shape[1]
    return pl.pallas_call(
        _ada_kernel,
        out_shape=jax.ShapeDtypeStruct((m, n), F32),
        grid=(n // tn,),
        in_specs=[_const_spec((m, d)), pl.BlockSpec((d, tn), lambda j: (0, j)),
                  pl.BlockSpec((1, tn), lambda j: (0, j))],
        out_specs=pl.BlockSpec((m, tn), lambda j: (0, j)),
        compiler_params=_cparams(("arbitrary",)),
        name="ada",
    )(c, w, b.reshape(1, n))


def _stage_tail(first_step, h, mix, gt_m, sh_f, sc_f, g_post, g_pre_ffn, w_router, b_router,
                cnt_ref, h_out, f_out, e_out, g_out, c_out, bb, tt):
    rows = bb * tt
    h1 = h + gt_m * _rms(mix, g_post)
    f = _rms(h1, g_pre_ffn) * (1.0 + sc_f) + sh_f
    h_out[...] = h1.reshape(bb, tt, -1)
    f_out[...] = f.reshape(bb, tt, -1)

    logits = _dot(f, w_router) + b_router
    lane = lax.broadcasted_iota(jnp.int32, logits.shape, 1)
    lane_k = lax.broadcasted_iota(jnp.int32, (rows, TOP_K), 1)
    work = logits
    multi_hot = jnp.zeros(logits.shape, F32)
    top_val, top_idx = [], []
    for _ in range(TOP_K):
        m = jnp.max(work, axis=-1, keepdims=True)
        idx = jnp.min(jnp.where(work == m, lane, N_EXPERTS), axis=-1, keepdims=True)
        sel = lane == idx
        multi_hot = multi_hot + sel.astype(F32)
        work = jnp.where(sel, -jnp.inf, work)
        top_val.append(m)
        top_idx.append(idx)
    exps = [jnp.exp(v - top_val[0]) for v in top_val]
    denom = exps[0] + exps[1] + exps[2] + exps[3]

    @pl.when(first_step)
    def _():
        cnt_ref[...] = jnp.zeros_like(cnt_ref)

    e_val = jnp.zeros((rows, TOP_K), jnp.int32)
    g_val = jnp.zeros((rows, TOP_K), F32)
    for k in range(TOP_K):
        e_val = jnp.where(lane_k == k, top_idx[k], e_val)
        g_val = jnp.where(lane_k == k, exps[k] / denom, g_val)
    e_out[...] = e_val.reshape(bb, tt, TOP_K)
    g_out[...] = g_val.reshape(bb, tt, TOP_K)
    cnt_ref[...] = cnt_ref[...] + jnp.sum(multi_hot, axis=0, keepdims=True)
    c_out[...] = cnt_ref[...]


def _tail_out_shapes(b, t, d):
    return [jax.ShapeDtypeStruct((b, t, d), F32), jax.ShapeDtypeStruct((b, t, d), F32),
            jax.ShapeDtypeStruct((b, t, TOP_K), jnp.int32), jax.ShapeDtypeStruct((b, t, TOP_K), F32),
            jax.ShapeDtypeStruct((1, N_EXPERTS), F32)]


def _tail_out_specs(bb, tt, d):
    tok = lambda n: pl.BlockSpec((bb, tt, n), lambda i, j: (i, j, 0))
    return [tok(d), tok(d), tok(TOP_K), tok(TOP_K), pl.BlockSpec((1, N_EXPERTS), lambda i, j: (0, 0))]


def _conv_stage_kernel(x_ref, mod_ref, prev_ref, g_pre, w_in, w_dw, w_out, g_post, g_pre_ffn,
                       w_router, b_router,
                       h_out, f_out, e_out, g_out, c_out, u_tail,
                       carry, cnt_ref, *, bb, tt, d):
    rows = bb * tt
    i, j = pl.program_id(0), pl.program_id(1)
    x = x_ref[...].reshape(rows, d)
    mv = lambda k: _modvec(mod_ref, k, bb, tt)
    a = _rms(x, g_pre[...]) * (1.0 + mv(1)) + mv(0)
    bcv = _dot(a, w_in[...])
    b_gate, c_gate, v = bcv[:, :d], bcv[:, d:2 * d], bcv[:, 2 * d:]
    u = c_gate * v

    tpos = lax.broadcasted_iota(jnp.int32, (rows, 1), 0) % tt
    if bb == 1:
        @pl.when(j == 0)
        def _():
            carry[...] = prev_ref[0]
        p0, p1 = carry[TAIL_ROWS - 2:TAIL_ROWS - 1, :], carry[TAIL_ROWS - 1:TAIL_ROWS, :]
        u_m1 = jnp.where(tpos == 0, p1, pltpu.roll(u, 1, 0))
        u_m2 = jnp.where(tpos == 0, p0, jnp.where(tpos == 1, p1, pltpu.roll(u, 2, 0)))
        carry[...] = u[rows - TAIL_ROWS:, :]
        u_tail[0] = u[rows - TAIL_ROWS:, :]
    else:
        pp = prev_ref[...].reshape(rows, d)
        u_m1 = jnp.where(tpos == 0, pltpu.roll(pp, rows - (TAIL_ROWS - 1), 0), pltpu.roll(u, 1, 0))
        u_m2 = jnp.where(tpos < 2, pltpu.roll(pp, rows - (TAIL_ROWS - 2), 0), pltpu.roll(u, 2, 0))
        u_tail[...] = u.reshape(bb, tt, d)
    conv = w_dw[0:1, :] * u_m2 + w_dw[1:2, :] * u_m1 + w_dw[2:3, :] * u
    mix = _dot(b_gate * conv, w_out[...])

    _stage_tail(jnp.logical_and(i == 0, j == 0), x, mix, mv(2), mv(3), mv(4), g_post[...], g_pre_ffn[...],
                w_router[...], b_router[...], cnt_ref, h_out, f_out, e_out, g_out, c_out, bb, tt)


def _conv_stage(x, mod, prev_pad, g_pre, w_in, w_dw, w_out, g_post, g_pre_ffn, w_router, b_router, bb, tt):
    b, t, d = x.shape
    assert b % bb == 0 and t % tt == 0 and (bb == 1 or tt == t == TAIL_ROWS)
    tok = lambda n: pl.BlockSpec((bb, tt, n), lambda i, j: (i, j, 0))
    seq = lambda n: pl.BlockSpec((bb, n, d), lambda i, j: (i, 0, 0))
    kern = functools.partial(_conv_stage_kernel, bb=bb, tt=tt, d=d)
    outs = pl.pallas_call(
        kern,
        out_shape=_tail_out_shapes(b, t, d) + [jax.ShapeDtypeStruct((b, TAIL_ROWS, d), F32)],
        grid=(b // bb, t // tt),
        in_specs=[tok(d), seq(6), seq(TAIL_ROWS), _const_spec((1, d)), _const_spec((d, 3 * d)),
                  _const_spec((CONV_W, d)), _const_spec((d, d)), _const_spec((1, d)), _const_spec((1, d)),
                  _const_spec((d, N_EXPERTS)), _const_spec((1, N_EXPERTS))],
        out_specs=_tail_out_specs(bb, tt, d) + [seq(TAIL_ROWS)],
        scratch_shapes=[pltpu.VMEM((TAIL_ROWS, d), F32), pltpu.VMEM((1, N_EXPERTS), F32)],
        compiler_params=_cparams(("arbitrary", "arbitrary")),
        name="conv_stage",
    )(x, mod, prev_pad, g_pre, w_in, w_dw, w_out, g_post, g_pre_ffn, w_router, b_router)
    return outs


RING = 3
LEAD_BLOCKS = RING


def _moe_kernel(asg_ref, be_ref, x_hbm, wgu, bgu, wd, bd, y_hbm, xbuf0, xbuf1, xbuf2, obuf0, obuf1, obuf2,
                wgu_b, wd_b, sem_in, sem_out, *, bm, d_ff, plane, tok_bits):
    i = pl.program_id(0)
    last = pl.num_programs(0) - 1
    xbufs, obufs = (xbuf0, xbuf1, xbuf2), (obuf0, obuf1, obuf2)

    def gather(blk, s):
        for r in range(bm):
            tok = jnp.bitwise_and(asg_ref[blk + LEAD_BLOCKS, r], (1 << tok_bits) - 1)
            pltpu.make_async_copy(x_hbm.at[pl.ds(tok, 1)], xbufs[s].at[pl.ds(r, 1)], sem_in.at[s]).start()

    def scatter(blk, s):
        for r in range(bm):
            v = asg_ref[blk + LEAD_BLOCKS, r]
            row = v - jnp.right_shift(v, tok_bits) * ((1 << tok_bits) - plane)
            pltpu.make_async_copy(obufs[s].at[pl.ds(r, 1)], y_hbm.at[pl.ds(row, 1)], sem_out.at[s]).start()

    def wait_gather(s):
        pltpu.make_async_copy(x_hbm.at[pl.ds(0, bm)], xbufs[s], sem_in.at[s]).wait()

    def wait_scatter(s):
        pltpu.make_async_copy(obufs[s], y_hbm.at[pl.ds(0, bm)], sem_out.at[s]).wait()

    @pl.when(i == 0)
    def _():
        for s in range(RING):
            obufs[s][...] = jnp.zeros_like(obufs[s])
        gather(0, 0)
        gather(jnp.minimum(1, last), 1)
        scatter(-3, 0)
        scatter(-2, 1)

    @pl.when(jnp.logical_or(i == 0, be_ref[i] != be_ref[jnp.maximum(i - 1, 0)]))
    def _():
        wgu_b[...] = wgu[0, 0].astype(BF16)
        wd_b[...] = wd[0, 0].astype(BF16)

    def body(s):
        prev = (s + RING - 1) % RING
        wait_gather(s)
        wait_scatter(s)
        hgu = _dot(xbufs[s][...], wgu_b[...]) + bgu[0, 0]
        gate = jnp.minimum(hgu[:, :d_ff], SWIGLU_LIMIT)
        up = jnp.clip(hgu[:, d_ff:], -SWIGLU_LIMIT, SWIGLU_LIMIT)
        glu = gate * jax.nn.sigmoid(gate * SWIGLU_ALPHA)
        obufs[s][...] = _dot((up + 1.0) * glu, wd_b[...]) + bd[0, 0]
        gather(jnp.minimum(i + 2, last), prev)
        scatter(i - 1, prev)

        @pl.when(i == last)
        def _():
            scatter(i, s)
            for q in range(RING):
                wait_scatter(q)
            wait_gather((s + 1) % RING)
            wait_gather(prev)

    for s in range(RING):
        pl.when(i % RING == s)(functools.partial(body, s))


def _moe(x2d, asg, block_e, tok_bits, layer, wgu, bgu, wd, bd, bm):
    plane, d = x2d.shape
    n_blocks = asg.shape[0] - LEAD_BLOCKS
    d_ff = wd.shape[2]
    n_out = TOP_K * plane
    kern = functools.partial(_moe_kernel, bm=bm, d_ff=d_ff, plane=plane, tok_bits=tok_bits)
    grid_spec = pltpu.PrefetchScalarGridSpec(
        num_scalar_prefetch=2,
        grid=(n_blocks,),
        in_specs=[pl.BlockSpec(memory_space=pl.ANY),
                  pl.BlockSpec((1, 1, d, 2 * d_ff), lambda i, t, e: (layer, e[i], 0, 0)),
                  pl.BlockSpec((1, 1, 1, 2 * d_ff), lambda i, t, e: (layer, e[i], 0, 0)),
                  pl.BlockSpec((1, 1, d_ff, d), lambda i, t, e: (layer, e[i], 0, 0)),
                  pl.BlockSpec((1, 1, 1, d), lambda i, t, e: (layer, e[i], 0, 0))],
        out_specs=pl.BlockSpec(memory_space=pl.ANY),
        scratch_shapes=[pltpu.VMEM((bm, d), F32)] * (2 * RING)
                       + [pltpu.VMEM((d, 2 * d_ff), BF16), pltpu.VMEM((d_ff, d), BF16),
                          pltpu.SemaphoreType.DMA((RING,)), pltpu.SemaphoreType.DMA((RING,))],
    )
    return pl.pallas_call(
        kern,
        out_shape=jax.ShapeDtypeStruct((n_out, d), F32),
        grid_spec=grid_spec,
        compiler_params=_cparams(("arbitrary",)),
        name="moe",
    )(asg, block_e, x2d, wgu, bgu, wd, bd)


def _combine_kernel(y_ref, gates_ref, h_ref, mod_ref, g_post, o_ref, *, bb, tt, d):
    rows = bb * tt
    g = gates_ref[...].reshape(rows, TOP_K)
    f = g[:, 0:1] * y_ref[0]
    for k in range(1, TOP_K):
        f = f + g[:, k:k + 1] * y_ref[k]
    h = h_ref[...].reshape(rows, d)
    gt_f = _modvec(mod_ref, 5, bb, tt)
    o_ref[...] = (h + gt_f * _rms(f, g_post[...])).reshape(bb, tt, d)


def _combine(y_tok, tok0, gates, h1, mod, g_post, bb, tt):
    b, t, d = h1.shape
    n_j = t // tt
    rows = bb * tt
    assert tok0 % rows == 0
    blk0 = tok0 // rows
    tok = lambda n: pl.BlockSpec((bb, tt, n), lambda i, j: (i, j, 0))
    kern = functools.partial(_combine_kernel, bb=bb, tt=tt, d=d)
    return pl.pallas_call(
        kern,
        out_shape=jax.ShapeDtypeStruct((b, t, d), F32),
        grid=(b // bb, n_j),
        in_specs=[pl.BlockSpec((TOP_K, rows, d), lambda i, j: (0, blk0 + i * n_j + j, 0)), tok(TOP_K), tok(d),
                  pl.BlockSpec((bb, 6, d), lambda i, j: (i, 0, 0)), _const_spec((1, d))],
        out_specs=tok(d),
        compiler_params=_cparams(("arbitrary", "arbitrary")),
        name="combine",
    )(y_tok, gates, h1, mod, g_post)


def _mla_pre_kernel(h_ref, mod_ref, modkv_ref, g_pre, g_kv_in, w_dkv, g_kv_lat, ck_ref, sk_ref,
                    w_dq, g_q, w_uq, w_ukt, cq_ref, sq_ref,
                    ckv_out, kr_out, ql_out, qr_out, *, bb, tt, d):
    rows = bb * tt
    h = h_ref[...].reshape(rows, d)
    ms = jnp.mean(h * h, axis=-1, keepdims=True)
    hn = h * lax.rsqrt(ms + RMS_EPS)

    kv_in = (hn * g_kv_in[...]) * (1.0 + _modvec(modkv_ref, 1, bb, tt)) + _modvec(modkv_ref, 0, bb, tt)
    lat = _dot(kv_in, w_dkv[...])
    ckv_out[...] = _rms(lat[:, :KV_RANK], g_kv_lat[...]).reshape(bb, tt, KV_RANK)
    kx = lat[:, KV_RANK:KV_RANK + ROPE_DIM].reshape(bb, tt, ROPE_DIM)
    kxs = lat[:, KV_RANK + ROPE_DIM:].reshape(bb, tt, ROPE_DIM)
    kr_out[...] = kx * ck_ref[...][None] + kxs * sk_ref[...][None]

    a = (hn * g_pre[...]) * (1.0 + _modvec(mod_ref, 1, bb, tt)) + _modvec(mod_ref, 0, bb, tt)
    qc = _rms(_dot(a, w_dq[...]), g_q[...])
    q = _dot(qc, w_uq[...])
    n_nope, n_rope = N_HEADS * NOPE_DIM, N_HEADS * ROPE_DIM
    qx = q[:, n_nope:n_nope + n_rope].reshape(bb, tt, n_rope)
    qxs = q[:, n_nope + n_rope:].reshape(bb, tt, n_rope)
    q_rope = qx * cq_ref[...][None] + qxs * sq_ref[...][None]
    for hd in range(N_HEADS):
        q_lat = _dot(q[:, hd * NOPE_DIM:(hd + 1) * NOPE_DIM], w_ukt[hd])
        ql_out[:, hd] = q_lat.reshape(bb, tt, KV_RANK).astype(ql_out.dtype)
        qr_out[:, hd] = q_rope[:, :, hd * ROPE_DIM:(hd + 1) * ROPE_DIM].astype(qr_out.dtype)


def _mla_pre(h, mod, modkv, g_pre, g_kv_in, w_dkv, g_kv_lat, ck, sk, w_dq, g_q, w_uq, w_ukt, cq, sq, bb, tt,
             q_dtype):
    b, t, d = h.shape
    q_rank = w_dq.shape[1]
    tok = lambda n: pl.BlockSpec((bb, tt, n), lambda i, j: (i, j, 0))
    head = lambda n: pl.BlockSpec((bb, N_HEADS, tt, n), lambda i, j: (i, 0, j, 0))
    seq = lambda n: pl.BlockSpec((bb, n, d), lambda i, j: (i, 0, 0))
    pos = lambda n: pl.BlockSpec((tt, n), lambda i, j: (j, 0))
    kern = functools.partial(_mla_pre_kernel, bb=bb, tt=tt, d=d)
    return pl.pallas_call(
        kern,
        out_shape=[jax.ShapeDtypeStruct((b, t, KV_RANK), F32), jax.ShapeDtypeStruct((b, t, ROPE_DIM), F32),
                   jax.ShapeDtypeStruct((b, N_HEADS, t, KV_RANK), q_dtype),
                   jax.ShapeDtypeStruct((b, N_HEADS, t, ROPE_DIM), q_dtype)],
        grid=(b // bb, t // tt),
        in_specs=[tok(d), seq(6), seq(2), _const_spec((1, d)), _const_spec((1, d)),
                  _const_spec(w_dkv.shape), _const_spec((1, KV_RANK)), pos(ROPE_DIM), pos(ROPE_DIM),
                  _const_spec(w_dq.shape), _const_spec((1, q_rank)), _const_spec(w_uq.shape),
                  _const_spec(w_ukt.shape), pos(N_HEADS * ROPE_DIM), pos(N_HEADS * ROPE_DIM)],
        out_specs=[tok(KV_RANK), tok(ROPE_DIM), head(KV_RANK), head(ROPE_DIM)],
        compiler_params=_cparams(("arbitrary", "arbitrary")),
        name="mla_pre",
    )(h, mod, modkv, g_pre, g_kv_in, w_dkv, g_kv_lat, ck, sk, w_dq, g_q, w_uq, w_ukt, cq, sq)


LANES = 128


def _softmax_step(s, m_prev, l_prev):
    keys = s.shape[1]
    m_new = jnp.maximum(m_prev, jnp.max(s, axis=-1, keepdims=True))
    alpha = jnp.exp(m_prev - m_new)
    p = jnp.exp(s - (jnp.tile(m_new, (1, keys // LANES)) if keys >= LANES else m_new[:, :keys]))
    l_new = alpha * l_prev + jnp.sum(p, axis=-1, keepdims=True)
    return p, m_new, l_new, alpha


def _attn_prompt_kernel(ql_ref, qr_ref, ckv_ref, kr_ref, o_ref, m_ref, l_ref, acc_ref, *, tq):
    i, j = pl.program_id(1), pl.program_id(2)
    rows = N_HEADS * tq

    @pl.when(j == 0)
    def _():
        m_ref[...] = jnp.full_like(m_ref, -jnp.inf)
        l_ref[...] = jnp.zeros_like(l_ref)
        acc_ref[...] = jnp.zeros_like(acc_ref)

    def step(masked):
        ckv = ckv_ref[0].astype(BF16)
        kr = kr_ref[0].astype(BF16)
        q = ql_ref[0].reshape(rows, KV_RANK)
        qr = qr_ref[0].reshape(rows, ROPE_DIM)
        s = (_dot_nt(q, ckv) + _dot_nt(qr, kr)) * SM_SCALE
        if masked:
            q_pos = lax.broadcasted_iota(jnp.int32, (rows, tq), 0) % tq
            k_pos = lax.broadcasted_iota(jnp.int32, (rows, tq), 1)
            s = jnp.where(k_pos <= q_pos, s, NEG_INF)
        p, m_new, l_new, alpha = _softmax_step(s, m_ref[...], l_ref[...])
        m_ref[...] = m_new
        l_ref[...] = l_new
        acc_ref[...] = (jnp.tile(alpha, (1, KV_RANK // LANES)) * acc_ref[...]
                        + jnp.dot(p.astype(BF16), ckv, preferred_element_type=F32))

    @pl.when(j < i)
    def _():
        step(False)

    @pl.when(j == i)
    def _():
        step(True)
        out = acc_ref[...] / jnp.tile(l_ref[...], (1, KV_RANK // LANES))
        o_ref[0] = out.reshape(N_HEADS, tq, KV_RANK).astype(o_ref.dtype)


def _attn_prompt(q_lat, q_rope, c_kv, k_rope, tq):
    b, _, t, _ = q_lat.shape
    n = t // tq
    rows = N_HEADS * tq
    qspec = lambda w: pl.BlockSpec((1, N_HEADS, tq, w), lambda bi, i, j: (bi, 0, i, 0))
    kspec = lambda w: pl.BlockSpec((1, tq, w), lambda bi, i, j: (bi, jnp.minimum(i, j), 0))
    return pl.pallas_call(
        functools.partial(_attn_prompt_kernel, tq=tq),
        out_shape=jax.ShapeDtypeStruct((b, N_HEADS, t, KV_RANK), BF16),
        grid=(b, n, n),
        in_specs=[qspec(KV_RANK), qspec(ROPE_DIM), kspec(KV_RANK), kspec(ROPE_DIM)],
        out_specs=qspec(KV_RANK),
        scratch_shapes=[pltpu.VMEM((rows, LANES), F32), pltpu.VMEM((rows, LANES), F32),
                        pltpu.VMEM((rows, KV_RANK), F32)],
        compiler_params=_cparams(("arbitrary", "arbitrary", "arbitrary")),
        name="attn_prompt",
    )(q_lat, q_rope, c_kv, k_rope)


def _attn_paged_kernel(pt_ref, ql_ref, qr_ref, cnew_ref, krnew_ref, *rest, ppc, spb, t_new):
    lat_pages = rest[:spb * ppc]
    rope_pages = rest[spb * ppc:2 * spb * ppc]
    o_ref = rest[2 * spb * ppc]
    kbuf, m_ref, l_ref, acc_ref = rest[2 * spb * ppc + 1:]
    j = pl.program_id(1)
    rows = N_HEADS * t_new

    @pl.when(j == 0)
    def _():
        m_ref[...] = jnp.full_like(m_ref, -jnp.inf)
        l_ref[...] = jnp.zeros_like(l_ref)
        acc_ref[...] = jnp.zeros_like(acc_ref)

    for sq in range(spb):
        q = ql_ref[sq].reshape(rows, KV_RANK).astype(BF16)
        qr = qr_ref[sq].reshape(rows, ROPE_DIM).astype(BF16)
        s_pages = []
        for k in range(ppc):
            kb = lat_pages[sq * ppc + k][0].astype(BF16)
            kbuf[sq, k * PAGE_SIZE:(k + 1) * PAGE_SIZE, :] = kb
            s_pages.append(_dot_nt(q, kb) + _dot(qr, rope_pages[sq * ppc + k][0]))
        s = jnp.concatenate(s_pages, axis=1) * SM_SCALE
        p, m_new, l_new, alpha = _softmax_step(s, m_ref[sq], l_ref[sq])
        m_ref[sq] = m_new
        l_ref[sq] = l_new
        acc_ref[sq] = (jnp.tile(alpha, (1, KV_RANK // LANES)) * acc_ref[sq]
                       + jnp.dot(p.astype(BF16), kbuf[sq], preferred_element_type=F32))

    @pl.when(j == pl.num_programs(1) - 1)
    def _():
        for sq in range(spb):
            q = ql_ref[sq].reshape(rows, KV_RANK).astype(BF16)
            qr = qr_ref[sq].reshape(rows, ROPE_DIM).astype(BF16)
            c_new = cnew_ref[sq].astype(BF16)
            s_new = (_dot_nt(q, c_new) + _dot_nt(qr, krnew_ref[sq])) * SM_SCALE
            q_pos = lax.broadcasted_iota(jnp.int32, (rows, t_new), 0) % t_new
            k_pos = lax.broadcasted_iota(jnp.int32, (rows, t_new), 1)
            s_new = jnp.where(k_pos <= q_pos, s_new, NEG_INF)
            p, m_new, l_new, alpha = _softmax_step(s_new, m_ref[sq], l_ref[sq])
            acc = (jnp.tile(alpha, (1, KV_RANK // LANES)) * acc_ref[sq]
                   + jnp.dot(p.astype(BF16), c_new, preferred_element_type=F32))
            out = acc / jnp.tile(l_new, (1, KV_RANK // LANES))
            o_ref[sq] = out.reshape(N_HEADS, t_new, KV_RANK)


def _attn_paged(q_lat, q_rope, c_new, kr_new, cache_lat, cache_rope_t, page_table, ppc, spb):
    s, _, t_new, _ = q_lat.shape
    n_pages = page_table.shape[1]
    assert n_pages % ppc == 0 and s % spb == 0
    rows = N_HEADS * t_new
    head = lambda w: pl.BlockSpec((spb, N_HEADS, t_new, w), lambda b, j, pt: (b, 0, 0, 0))
    seq = lambda w: pl.BlockSpec((spb, t_new, w), lambda b, j, pt: (b, 0, 0))
    lat_page = lambda sq, k: pl.BlockSpec((1, PAGE_SIZE, KV_RANK),
                                          lambda b, j, pt: (pt[b * spb + sq, j * ppc + k], 0, 0))
    rope_page = lambda sq, k: pl.BlockSpec((1, ROPE_DIM, PAGE_SIZE),
                                           lambda b, j, pt: (pt[b * spb + sq, j * ppc + k], 0, 0))
    pairs = [(sq, k) for sq in range(spb) for k in range(ppc)]
    grid_spec = pltpu.PrefetchScalarGridSpec(
        num_scalar_prefetch=1,
        grid=(s // spb, n_pages // ppc),
        in_specs=[head(KV_RANK), head(ROPE_DIM), seq(KV_RANK), seq(ROPE_DIM)]
                 + [lat_page(sq, k) for sq, k in pairs] + [rope_page(sq, k) for sq, k in pairs],
        out_specs=head(KV_RANK),
        scratch_shapes=[pltpu.VMEM((spb, ppc * PAGE_SIZE, KV_RANK), BF16),
                        pltpu.VMEM((spb, rows, LANES), F32), pltpu.VMEM((spb, rows, LANES), F32),
                        pltpu.VMEM((spb, rows, KV_RANK), F32)],
    )
    n_in = spb * ppc
    return pl.pallas_call(
        functools.partial(_attn_paged_kernel, ppc=ppc, spb=spb, t_new=t_new),
        out_shape=jax.ShapeDtypeStruct((s, N_HEADS, t_new, KV_RANK), F32),
        grid_spec=grid_spec,
        compiler_params=_cparams(("arbitrary", "arbitrary")),
        name="attn_paged",
    )(page_table, q_lat, q_rope, c_new, kr_new, *([cache_lat] * n_in), *([cache_rope_t] * n_in))


def _mla_post_kernel(h_ref, ol_ref, mod_ref, w_uv, w_o, g_post, g_pre_ffn, w_router, b_router,
                     h_out, f_out, e_out, g_out, c_out, obuf, cnt_ref, *, bb, tt, d):
    rows = bb * tt
    i, j = pl.program_id(0), pl.program_id(1)
    h = h_ref[...].reshape(rows, d)
    for hd in range(N_HEADS):
        o_lat = ol_ref[:, hd].reshape(rows, KV_RANK)
        obuf[:, hd * V_DIM:(hd + 1) * V_DIM] = _dot(o_lat, w_uv[hd]).astype(BF16)
    mix = jnp.dot(obuf[...], w_o[...], preferred_element_type=F32)
    mv = lambda k: _modvec(mod_ref, k, bb, tt)
    _stage_tail(jnp.logical_and(i == 0, j == 0), h, mix, mv(2), mv(3), mv(4), g_post[...], g_pre_ffn[...],
                w_router[...], b_router[...], cnt_ref, h_out, f_out, e_out, g_out, c_out, bb, tt)


def _mla_post(h, o_lat, mod, w_uv, w_o, g_post, g_pre_ffn, w_router, b_router, bb, tt):
    b, t, d = h.shape
    tok = lambda n: pl.BlockSpec((bb, tt, n), lambda i, j: (i, j, 0))
    kern = functools.partial(_mla_post_kernel, bb=bb, tt=tt, d=d)
    return pl.pallas_call(
        kern,
        out_shape=_tail_out_shapes(b, t, d),
        grid=(b // bb, t // tt),
        in_specs=[tok(d), pl.BlockSpec((bb, N_HEADS, tt, KV_RANK), lambda i, j: (i, 0, j, 0)),
                  pl.BlockSpec((bb, 6, d), lambda i, j: (i, 0, 0)),
                  _const_spec(w_uv.shape), _const_spec(w_o.shape), _const_spec((1, d)), _const_spec((1, d)),
                  _const_spec((d, N_EXPERTS)), _const_spec((1, N_EXPERTS))],
        out_specs=_tail_out_specs(bb, tt, d),
        scratch_shapes=[pltpu.VMEM((bb * tt, N_HEADS * V_DIM), BF16), pltpu.VMEM((1, N_EXPERTS), F32)],
        compiler_params=_cparams(("arbitrary", "arbitrary")),
        name="mla_post",
    )(h, o_lat, mod, w_uv, w_o, g_post, g_pre_ffn, w_router, b_router)


def _moe_layer(f, top_e, gates, counts, h1, mod, g_post, layer, wgu, bgu, wd, bd, bm, bb, tt):
    d = f[0].shape[-1]
    n_tok = [x.shape[0] * x.shape[1] for x in f]
    starts = [sum(n_tok[:g]) for g in range(len(f))]
    n_real = sum(n_tok)
    n_assign = n_real * TOP_K
    n_blocks = (n_assign + N_EXPERTS * (bm - 1) + bm - 1) // bm
    counts = sum(c.reshape(N_EXPERTS).astype(jnp.int32) for c in counts)
    padded = (counts + bm - 1) // bm * bm
    pad_end = jnp.cumsum(padded)
    pad_start = pad_end - padded
    sorted_start = jnp.cumsum(counts) - counts
    order = jnp.argsort(jnp.concatenate([e.reshape(-1) for e in top_e]), stable=True).astype(jnp.int32)

    n_dummy_tok = RING * bm // TOP_K
    plane = n_real + n_dummy_tok
    tok_bits = (plane - 1).bit_length()
    rows = jnp.arange(n_blocks + LEAD_BLOCKS, dtype=jnp.int32)[:, None]
    cols = jnp.arange(bm, dtype=jnp.int32)[None, :]
    dummy = (rows % RING) * bm + cols
    dummy = ((dummy % TOP_K) << tok_bits) + n_real + dummy // TOP_K
    block_start = jnp.arange(n_blocks, dtype=jnp.int32) * bm
    block_e = jnp.minimum(jnp.sum(pad_end[None, :] <= block_start[:, None], axis=1), N_EXPERTS - 1).astype(jnp.int32)
    within = block_start[:, None] + cols - pad_start[block_e][:, None]
    valid = within < counts[block_e][:, None]
    a = order[jnp.clip(sorted_start[block_e][:, None] + within, 0, n_assign - 1)]
    real = ((a % TOP_K) << tok_bits) + a // TOP_K
    asg = jnp.concatenate([dummy[:LEAD_BLOCKS], jnp.where(valid, real, dummy[LEAD_BLOCKS:])], axis=0)

    x2d = jnp.concatenate([x.reshape(n, d) for x, n in zip(f, n_tok)] + [jnp.zeros((n_dummy_tok, d), F32)], axis=0)
    y = _moe(x2d, asg, block_e, tok_bits, layer, wgu, bgu, wd, bd, bm)
    y_tok = y.reshape(TOP_K, plane, d)
    return [_combine(y_tok, starts[g], gates[g], h1[g], mod[g], g_post, bb[g], tt[g]) for g in range(len(f))]


def _rope_tables(pos):
    inv = jnp.exp(-jnp.log(ROPE_THETA) * jnp.arange(HALF_ROPE, dtype=jnp.float32) / HALF_ROPE)
    ang = pos.astype(jnp.float32)[:, None] * inv[None, :]
    cos, sin = jnp.cos(ang), jnp.sin(ang)
    ck = jnp.concatenate([cos, cos], axis=-1)
    sk = jnp.concatenate([-sin, sin], axis=-1)
    return ck, sk, jnp.tile(ck, (1, N_HEADS)), jnp.tile(sk, (1, N_HEADS))


def _trunk(groups, w, bm):
    row = lambda v: v.reshape(1, -1)
    col = lambda outs, k: [o[k] for o in outs]

    def moe(stage, layer):
        return _moe_layer(col(stage, 1), col(stage, 2), col(stage, 3), col(stage, 4), col(stage, 0),
                          [g['mods'][layer] for g in groups], row(w['g_post_ffn'][layer]), layer,
                          w['w_gate_up'], w['b_gate_up'], w['w_down'], w['b_down'], bm,
                          [g['tiles']['cbb'] for g in groups], [g['tiles']['ctt'] for g in groups])

    stage = [_conv_stage(
        g['x'], g['mods'][0], g['prev'], row(w['g_pre_mix'][0]), w['w_conv_in'], w['w_conv_dw'], w['w_conv_out'],
        row(w['g_post_mix'][0]), row(w['g_pre_ffn'][0]), w['w_router'][0], row(w['b_router'][0]),
        g['tiles']['bb'], g['tiles']['tt']) for g in groups]
    u_tails = col(stage, 5)
    hs = moe(stage, 0)

    stage, lat, rope = [], [], []
    for g, h in zip(groups, hs):
        t = g['tiles']
        ck, sk, cq, sq = _rope_tables(g['pos'])
        c_kv, k_rope, q_lat, q_rope = _mla_pre(
            h, g['mods'][1], g['modkv'], row(w['g_pre_mix'][1]), row(w['g_kv_in']), w['w_dkv'], row(w['g_kv_lat']),
            ck, sk, w['w_dq'], row(w['g_q']), w['w_uq'], w['w_ukt'], cq, sq, t['bb'], t['tt'], t['q_dtype'])
        o_lat = g['attend'](q_lat, q_rope, c_kv, k_rope)
        stage.append(_mla_post(h, o_lat, g['mods'][1], w['w_uv'], w['w_o'], row(w['g_post_mix'][1]),
                               row(w['g_pre_ffn'][1]), w['w_router'][1], row(w['b_router'][1]), t['bb'], t['tt']))
        lat.append(c_kv)
        rope.append(k_rope)
    ys = moe(stage, 1)
    conv = [u[None, :, TAIL_ROWS - (CONV_W - 1):, :] for u in u_tails]
    return ys, conv, lat, rope


def _prepare_weights(p):
    w = dict(p)
    w['w_conv_in'] = p['w_conv_in'][0].astype(BF16)
    w['w_conv_dw'] = p['w_conv_dw'][0]
    w['w_conv_out'] = p['w_conv_out'][0].astype(BF16)
    wk = p['w_dkv']
    w['w_dkv'] = jnp.concatenate([wk, wk[:, KV_RANK + HALF_ROPE:], wk[:, KV_RANK:KV_RANK + HALF_ROPE]],
                                 axis=1).astype(BF16)
    wq = p['w_uq'][0].reshape(-1, N_HEADS, NOPE_DIM + ROPE_DIM)
    q_rank = wq.shape[0]
    nope = wq[:, :, :NOPE_DIM].reshape(q_rank, N_HEADS * NOPE_DIM)
    rope = wq[:, :, NOPE_DIM:]
    rope_sw = jnp.concatenate([rope[:, :, HALF_ROPE:], rope[:, :, :HALF_ROPE]], axis=-1)
    w['w_uq'] = jnp.concatenate([nope, rope.reshape(q_rank, -1), rope_sw.reshape(q_rank, -1)], axis=1).astype(BF16)
    w['w_dq'] = p['w_dq'][0].astype(BF16)
    w['g_q'] = p['g_q'][0]
    w['w_ukt'] = jnp.transpose(p['w_uk'], (1, 2, 0)).astype(BF16)
    w['w_uv'] = jnp.transpose(p['w_uv'], (1, 0, 2)).astype(BF16)
    w['w_o'] = p['w_o'][0].astype(BF16)
    w['w_router'] = p['w_router'].astype(BF16)
    w['b_gate_up'] = p['b_gate_up'][:, :, None, :]
    w['b_down'] = p['b_down'][:, :, None, :]
    return w


def kernel(x_prompt, x_sample, cache_kv_latent, cache_k_rope, state_conv, page_table, c_prompt, c_sample,
           w_ada, b_ada, g_pre_mix, g_post_mix, g_pre_ffn, g_post_ffn, w_conv_in, w_conv_dw, w_conv_out,
           w_ada_kv, b_ada_kv, g_kv_in, w_dkv, g_kv_lat, w_uk, w_uv, w_dq, g_q, w_uq, w_o,
           w_router, b_router, w_gate_up, b_gate_up, w_down, b_down):
    p = dict(g_pre_mix=g_pre_mix, g_post_mix=g_post_mix, g_pre_ffn=g_pre_ffn, g_post_ffn=g_post_ffn,
             w_conv_in=w_conv_in, w_conv_dw=w_conv_dw, w_conv_out=w_conv_out, g_kv_in=g_kv_in, w_dkv=w_dkv,
             g_kv_lat=g_kv_lat, w_uk=w_uk, w_uv=w_uv, w_dq=w_dq, g_q=g_q, w_uq=w_uq, w_o=w_o,
             w_router=w_router, b_router=b_router, w_gate_up=w_gate_up, b_gate_up=b_gate_up,
             w_down=w_down, b_down=b_down)
    w = _prepare_weights(p)
    bp, tp, d = x_prompt.shape
    bs, ts, _ = x_sample.shape
    depth = w_ada.shape[0]

    c_all = jnp.concatenate([c_prompt, c_sample], axis=0)
    mods = [_ada(c_all, w_ada[l], b_ada[l]).reshape(bp + bs, 6, d) for l in range(depth)]
    modkv = _ada(c_all, w_ada_kv, b_ada_kv).reshape(bp + bs, 2, d)

    pos_p = jnp.arange(tp, dtype=jnp.int32)
    prev_p = jnp.zeros((bp, TAIL_ROWS, d), F32)
    tiles_p = dict(bb=1, tt=min(tp, 512), cbb=1, ctt=min(tp, 256), q_dtype=BF16)
    attend_p = lambda ql, qr, ckv, kr: _attn_prompt(ql, qr, ckv, kr, min(tp, 512))
    prompt = dict(x=x_prompt, mods=[m[:bp] for m in mods], modkv=modkv[:bp], pos=pos_p, prev=prev_p,
                  attend=attend_p, tiles=tiles_p)

    n_pages = page_table.shape[1]
    pos_s = n_pages * PAGE_SIZE + jnp.arange(ts, dtype=jnp.int32)
    prev_s = jnp.concatenate([jnp.zeros((bs, TAIL_ROWS - (CONV_W - 1), d), F32), state_conv[0]], axis=1)
    sbb = min(bs, 32)
    tiles_s = dict(bb=sbb, tt=ts, cbb=sbb, ctt=ts, q_dtype=F32)
    ppc = math.gcd(n_pages, 16)
    spb = math.gcd(bs, 2)
    cache_rope_t = jnp.swapaxes(cache_k_rope, 1, 2)
    attend_s = lambda ql, qr, ckv, kr: _attn_paged(ql, qr, ckv, kr, cache_kv_latent, cache_rope_t, page_table,
                                                   ppc, spb)
    sample = dict(x=x_sample, mods=[m[bp:] for m in mods], modkv=modkv[bp:], pos=pos_s, prev=prev_s,
                  attend=attend_s, tiles=tiles_s)

    (y_p, y_s), (conv_p, conv_s), (lat_p, lat_s), (rope_p, rope_s) = _trunk([prompt, sample], w, MOE_BLOCK_ROWS)
    return (y_p, y_s, lat_p, rope_p, conv_p, lat_s, rope_s, conv_s)
```

```python
import functools
import math

import jax
import jax.numpy as jnp
import numpy as np
from jax import lax
from jax.experimental import pallas as pl
from jax.experimental.pallas import tpu as pltpu

N_HEADS = 8
NOPE_DIM = 128
ROPE_DIM = 64
HALF_ROPE = ROPE_DIM // 2
V_DIM = 128
KV_RANK = 256
ROPE_THETA = 10000.0
SM_SCALE = (NOPE_DIM + ROPE_DIM) ** -0.5
N_EXPERTS = 32
TOP_K = 4
SWIGLU_LIMIT = 7.0
SWIGLU_ALPHA = 1.702
RMS_EPS = 1e-6
NEG_INF = -1e30
PAGE_SIZE = 128
CONV_W = 3
TAIL_ROWS = 8
MOE_BLOCK_ROWS = 256

VMEM_LIMIT = 56 * 1024 * 1024
BF16 = jnp.bfloat16
F32 = jnp.float32


def _cparams(sem):
    return pltpu.CompilerParams(dimension_semantics=sem, vmem_limit_bytes=VMEM_LIMIT)


def _const_spec(shape):
    nd = len(shape)
    return pl.BlockSpec(shape, lambda *_: (0,) * nd)


def _rms(x, g):
    ms = jnp.mean(x * x, axis=-1, keepdims=True)
    return x * lax.rsqrt(ms + RMS_EPS) * g


def _dot(a, b):
    return jnp.dot(a.astype(BF16), b.astype(BF16), preferred_element_type=F32)


def _dot_nt(a, b):
    return lax.dot_general(a.astype(BF16), b.astype(BF16), (((1,), (1,)), ((), ())),
                           preferred_element_type=F32)


def _modvec(mod_ref, j, bb, tt):
    v = mod_ref[:, j:j + 1, :]
    d = v.shape[-1]
    if bb == 1:
        return v[0]
    return jnp.broadcast_to(v, (bb, tt, d)).reshape(bb * tt, d)


def _ada_kernel(c_ref, w_ref, b_ref, o_ref):
    c = c_ref[...]
    c_act = c * jax.nn.sigmoid(c)
    o_ref[...] = _dot(c_act, w_ref[...]) + b_ref[...]


def _ada(c, w, b, tn=1024):
    m, d = c.shape
    n = w.shape[1]
    return pl.pallas_call(
        _ada_kernel,
        out_shape=jax.ShapeDtypeStruct((m, n), F32),
        grid=(n // tn,),
        in_specs=[_const_spec((m, d)), pl.BlockSpec((d, tn), lambda j: (0, j)),
                  pl.BlockSpec((1, tn), lambda j: (0, j))],
        out_specs=pl.BlockSpec((m, tn), lambda j: (0, j)),
        compiler_params=_cparams(("arbitrary",)),
        name="ada",
    )(c, w, b.reshape(1, n))


def _stage_tail(first_step, h, mix, gt_m, sh_f, sc_f, g_post, g_pre_ffn, w_router, b_router,
                cnt_ref, h_out, f_out, e_out, g_out, c_out, bb, tt):
    rows = bb * tt
    h1 = h + gt_m * _rms(mix, g_post)
    f = _rms(h1, g_pre_ffn) * (1.0 + sc_f) + sh_f
    h_out[...] = h1.reshape(bb, tt, -1)
    f_out[...] = f.reshape(bb, tt, -1)

    logits = _dot(f, w_router) + b_router
    lane = lax.broadcasted_iota(jnp.int32, logits.shape, 1)
    lane_k = lax.broadcasted_iota(jnp.int32, (rows, TOP_K), 1)
    work = logits
    multi_hot = jnp.zeros(logits.shape, F32)
    top_val, top_idx = [], []
    for _ in range(TOP_K):
        m = jnp.max(work, axis=-1, keepdims=True)
        idx = jnp.min(jnp.where(work == m, lane, N_EXPERTS), axis=-1, keepdims=True)
        sel = lane == idx
        multi_hot = multi_hot + sel.astype(F32)
        work = jnp.where(sel, -jnp.inf, work)
        top_val.append(m)
        top_idx.append(idx)
    exps = [jnp.exp(v - top_val[0]) for v in top_val]
    denom = exps[0] + exps[1] + exps[2] + exps[3]

    @pl.when(first_step)
    def _():
        cnt_ref[...] = jnp.zeros_like(cnt_ref)

    e_val = jnp.zeros((rows, TOP_K), jnp.int32)
    g_val = jnp.zeros((rows, TOP_K), F32)
    for k in range(TOP_K):
        e_val = jnp.where(lane_k == k, top_idx[k], e_val)
        g_val = jnp.where(lane_k == k, exps[k] / denom, g_val)
    e_out[...] = e_val.reshape(bb, tt, TOP_K)
    g_out[...] = g_val.reshape(bb, tt, TOP_K)
    cnt_ref[...] = cnt_ref[...] + jnp.sum(multi_hot, axis=0, keepdims=True)
    c_out[...] = cnt_ref[...]


def _tail_out_shapes(b, t, d):
    return [jax.ShapeDtypeStruct((b, t, d), F32), jax.ShapeDtypeStruct((b, t, d), F32),
            jax.ShapeDtypeStruct((b, t, TOP_K), jnp.int32), jax.ShapeDtypeStruct((b, t, TOP_K), F32),
            jax.ShapeDtypeStruct((1, N_EXPERTS), F32)]


def _tail_out_specs(bb, tt, d):
    tok = lambda n: pl.BlockSpec((bb, tt, n), lambda i, j: (i, j, 0))
    return [tok(d), tok(d), tok(TOP_K), tok(TOP_K), pl.BlockSpec((1, N_EXPERTS), lambda i, j: (0, 0))]


def _conv_stage_kernel(x_ref, mod_ref, prev_ref, g_pre, w_in, w_dw, w_out, g_post, g_pre_ffn,
                       w_router, b_router,
                       h_out, f_out, e_out, g_out, c_out, u_tail,
                       carry, cnt_ref, *, bb, tt, d):
    rows = bb * tt
    i, j = pl.program_id(0), pl.program_id(1)
    x = x_ref[...].reshape(rows, d)
    mv = lambda k: _modvec(mod_ref, k, bb, tt)
    a = _rms(x, g_pre[...]) * (1.0 + mv(1)) + mv(0)
    bcv = _dot(a, w_in[...])
    b_gate, c_gate, v = bcv[:, :d], bcv[:, d:2 * d], bcv[:, 2 * d:]
    u = c_gate * v

    tpos = lax.broadcasted_iota(jnp.int32, (rows, 1), 0) % tt
    if bb == 1:
        @pl.when(j == 0)
        def _():
            carry[...] = prev_ref[0]
        p0, p1 = carry[TAIL_ROWS - 2:TAIL_ROWS - 1, :], carry[TAIL_ROWS - 1:TAIL_ROWS, :]
        u_m1 = jnp.where(tpos == 0, p1, pltpu.roll(u, 1, 0))
        u_m2 = jnp.where(tpos == 0, p0, jnp.where(tpos == 1, p1, pltpu.roll(u, 2, 0)))
        carry[...] = u[rows - TAIL_ROWS:, :]
        u_tail[0] = u[rows - TAIL_ROWS:, :]
    else:
        pp = prev_ref[...].reshape(rows, d)
        u_m1 = jnp.where(tpos == 0, pltpu.roll(pp, rows - (TAIL_ROWS - 1), 0), pltpu.roll(u, 1, 0))
        u_m2 = jnp.where(tpos < 2, pltpu.roll(pp, rows - (TAIL_ROWS - 2), 0), pltpu.roll(u, 2, 0))
        u_tail[...] = u.reshape(bb, tt, d)
    conv = w_dw[0:1, :] * u_m2 + w_dw[1:2, :] * u_m1 + w_dw[2:3, :] * u
    mix = _dot(b_gate * conv, w_out[...])

    _stage_tail(jnp.logical_and(i == 0, j == 0), x, mix, mv(2), mv(3), mv(4), g_post[...], g_pre_ffn[...],
                w_router[...], b_router[...], cnt_ref, h_out, f_out, e_out, g_out, c_out, bb, tt)


def _conv_stage(x, mod, prev_pad, g_pre, w_in, w_dw, w_out, g_post, g_pre_ffn, w_router, b_router, bb, tt):
    b, t, d = x.shape
    assert b % bb == 0 and t % tt == 0 and (bb == 1 or tt == t == TAIL_ROWS)
    tok = lambda n: pl.BlockSpec((bb, tt, n), lambda i, j: (i, j, 0))
    seq = lambda n: pl.BlockSpec((bb, n, d), lambda i, j: (i, 0, 0))
    kern = functools.partial(_conv_stage_kernel, bb=bb, tt=tt, d=d)
    outs = pl.pallas_call(
        kern,
        out_shape=_tail_out_shapes(b, t, d) + [jax.ShapeDtypeStruct((b, TAIL_ROWS, d), F32)],
        grid=(b // bb, t // tt),
        in_specs=[tok(d), seq(6), seq(TAIL_ROWS), _const_spec((1, d)), _const_spec((d, 3 * d)),
                  _const_spec((CONV_W, d)), _const_spec((d, d)), _const_spec((1, d)), _const_spec((1, d)),
                  _const_spec((d, N_EXPERTS)), _const_spec((1, N_EXPERTS))],
        out_specs=_tail_out_specs(bb, tt, d) + [seq(TAIL_ROWS)],
        scratch_shapes=[pltpu.VMEM((TAIL_ROWS, d), F32), pltpu.VMEM((1, N_EXPERTS), F32)],
        compiler_params=_cparams(("arbitrary", "arbitrary")),
        name="conv_stage",
    )(x, mod, prev_pad, g_pre, w_in, w_dw, w_out, g_post, g_pre_ffn, w_router, b_router)
    return outs


RING = 3
LEAD_BLOCKS = RING


def _moe_kernel(asg_ref, be_ref, x_hbm, wgu, bgu, wd, bd, y_hbm, xbuf0, xbuf1, xbuf2, obuf0, obuf1, obuf2,
                wgu_b, wd_b, sem_in, sem_out, *, bm, d_ff, plane, tok_bits):
    i = pl.program_id(0)
    last = pl.num_programs(0) - 1
    xbufs, obufs = (xbuf0, xbuf1, xbuf2), (obuf0, obuf1, obuf2)

    def gather(blk, s):
        for r in range(bm):
            tok = jnp.bitwise_and(asg_ref[blk + LEAD_BLOCKS, r], (1 << tok_bits) - 1)
            pltpu.make_async_copy(x_hbm.at[pl.ds(tok, 1)], xbufs[s].at[pl.ds(r, 1)], sem_in.at[s]).start()

    def scatter(blk, s):
        for r in range(bm):
            v = asg_ref[blk + LEAD_BLOCKS, r]
            row = v - jnp.right_shift(v, tok_bits) * ((1 << tok_bits) - plane)
            pltpu.make_async_copy(obufs[s].at[pl.ds(r, 1)], y_hbm.at[pl.ds(row, 1)], sem_out.at[s]).start()

    def wait_gather(s):
        pltpu.make_async_copy(x_hbm.at[pl.ds(0, bm)], xbufs[s], sem_in.at[s]).wait()

    def wait_scatter(s):
        pltpu.make_async_copy(obufs[s], y_hbm.at[pl.ds(0, bm)], sem_out.at[s]).wait()

    @pl.when(i == 0)
    def _():
        for s in range(RING):
            obufs[s][...] = jnp.zeros_like(obufs[s])
        gather(0, 0)
        gather(jnp.minimum(1, last), 1)
        scatter(-3, 0)
        scatter(-2, 1)

    @pl.when(jnp.logical_or(i == 0, be_ref[i] != be_ref[jnp.maximum(i - 1, 0)]))
    def _():
        wgu_b[...] = wgu[0, 0].astype(BF16)
        wd_b[...] = wd[0, 0].astype(BF16)

    def body(s):
        prev = (s + RING - 1) % RING
        wait_gather(s)
        gather(jnp.minimum(i + 2, last), prev)
        hgu = _dot(xbufs[s][...], wgu_b[...]) + bgu[0, 0]
        gate = jnp.minimum(hgu[:, :d_ff], SWIGLU_LIMIT)
        up = jnp.clip(hgu[:, d_ff:], -SWIGLU_LIMIT, SWIGLU_LIMIT)
        glu = gate * jax.nn.sigmoid(gate * SWIGLU_ALPHA)
        act = ((up + 1.0) * glu).astype(BF16)
        wait_scatter(s)
        scatter(i - 1, prev)
        obufs[s][...] = jnp.dot(act, wd_b[...], preferred_element_type=F32) + bd[0, 0]

        @pl.when(i == last)
        def _():
            scatter(i, s)
            for q in range(RING):
                wait_scatter(q)
            wait_gather((s + 1) % RING)
            wait_gather(prev)

    for s in range(RING):
        pl.when(i % RING == s)(functools.partial(body, s))


def _moe(x2d, asg, block_e, tok_bits, layer, wgu, bgu, wd, bd, bm):
    plane, d = x2d.shape
    n_blocks = asg.shape[0] - LEAD_BLOCKS
    d_ff = wd.shape[2]
    n_out = TOP_K * plane
    kern = functools.partial(_moe_kernel, bm=bm, d_ff=d_ff, plane=plane, tok_bits=tok_bits)
    grid_spec = pltpu.PrefetchScalarGridSpec(
        num_scalar_prefetch=2,
        grid=(n_blocks,),
        in_specs=[pl.BlockSpec(memory_space=pl.ANY),
                  pl.BlockSpec((1, 1, d, 2 * d_ff), lambda i, t, e: (layer, e[i], 0, 0)),
                  pl.BlockSpec((1, 1, 1, 2 * d_ff), lambda i, t, e: (layer, e[i], 0, 0)),
                  pl.BlockSpec((1, 1, d_ff, d), lambda i, t, e: (layer, e[i], 0, 0)),
                  pl.BlockSpec((1, 1, 1, d), lambda i, t, e: (layer, e[i], 0, 0))],
        out_specs=pl.BlockSpec(memory_space=pl.ANY),
        scratch_shapes=[pltpu.VMEM((bm, d), F32)] * (2 * RING)
                       + [pltpu.VMEM((d, 2 * d_ff), BF16), pltpu.VMEM((d_ff, d), BF16),
                          pltpu.SemaphoreType.DMA((RING,)), pltpu.SemaphoreType.DMA((RING,))],
    )
    return pl.pallas_call(
        kern,
        out_shape=jax.ShapeDtypeStruct((n_out, d), F32),
        grid_spec=grid_spec,
        compiler_params=_cparams(("arbitrary",)),
        name="moe",
    )(asg, block_e, x2d, wgu, bgu, wd, bd)


def _combine_kernel(y_ref, gates_ref, h_ref, mod_ref, g_post, o_ref, *, bb, tt, d):
    rows = bb * tt
    g = gates_ref[...].reshape(rows, TOP_K)
    f = g[:, 0:1] * y_ref[0]
    for k in range(1, TOP_K):
        f = f + g[:, k:k + 1] * y_ref[k]
    h = h_ref[...].reshape(rows, d)
    gt_f = _modvec(mod_ref, 5, bb, tt)
    o_ref[...] = (h + gt_f * _rms(f, g_post[...])).reshape(bb, tt, d)


def _combine(y_tok, tok0, gates, h1, mod, g_post, bb, tt):
    b, t, d = h1.shape
    n_j = t // tt
    rows = bb * tt
    assert tok0 % rows == 0
    blk0 = tok0 // rows
    tok = lambda n: pl.BlockSpec((bb, tt, n), lambda i, j: (i, j, 0))
    kern = functools.partial(_combine_kernel, bb=bb, tt=tt, d=d)
    return pl.pallas_call(
        kern,
        out_shape=jax.ShapeDtypeStruct((b, t, d), F32),
        grid=(b // bb, n_j),
        in_specs=[pl.BlockSpec((TOP_K, rows, d), lambda i, j: (0, blk0 + i * n_j + j, 0)), tok(TOP_K), tok(d),
                  pl.BlockSpec((bb, 6, d), lambda i, j: (i, 0, 0)), _const_spec((1, d))],
        out_specs=tok(d),
        compiler_params=_cparams(("arbitrary", "arbitrary")),
        name="combine",
    )(y_tok, gates, h1, mod, g_post)


def _mla_pre_kernel(h_ref, mod_ref, modkv_ref, g_pre, g_kv_in, w_dkv, g_kv_lat, ck_ref, sk_ref,
                    w_dq, g_q, w_uq, w_ukt, cq_ref, sq_ref,
                    ckv_out, kr_out, ql_out, qr_out, *, bb, tt, d):
    rows = bb * tt
    h = h_ref[...].reshape(rows, d)
    ms = jnp.mean(h * h, axis=-1, keepdims=True)
    hn = h * lax.rsqrt(ms + RMS_EPS)

    kv_in = (hn * g_kv_in[...]) * (1.0 + _modvec(modkv_ref, 1, bb, tt)) + _modvec(modkv_ref, 0, bb, tt)
    lat = _dot(kv_in, w_dkv[...])
    ckv_out[...] = _rms(lat[:, :KV_RANK], g_kv_lat[...]).reshape(bb, tt, KV_RANK)
    kx = lat[:, KV_RANK:KV_RANK + ROPE_DIM].reshape(bb, tt, ROPE_DIM)
    kxs = lat[:, KV_RANK + ROPE_DIM:].reshape(bb, tt, ROPE_DIM)
    kr_out[...] = kx * ck_ref[...][None] + kxs * sk_ref[...][None]

    a = (hn * g_pre[...]) * (1.0 + _modvec(mod_ref, 1, bb, tt)) + _modvec(mod_ref, 0, bb, tt)
    qc = _rms(_dot(a, w_dq[...]), g_q[...])
    q = _dot(qc, w_uq[...])
    n_nope, n_rope = N_HEADS * NOPE_DIM, N_HEADS * ROPE_DIM
    qx = q[:, n_nope:n_nope + n_rope].reshape(bb, tt, n_rope)
    qxs = q[:, n_nope + n_rope:].reshape(bb, tt, n_rope)
    q_rope = qx * cq_ref[...][None] + qxs * sq_ref[...][None]
    for hd in range(N_HEADS):
        q_lat = _dot(q[:, hd * NOPE_DIM:(hd + 1) * NOPE_DIM], w_ukt[hd])
        ql_out[:, hd] = q_lat.reshape(bb, tt, KV_RANK).astype(ql_out.dtype)
        qr_out[:, hd] = q_rope[:, :, hd * ROPE_DIM:(hd + 1) * ROPE_DIM].astype(qr_out.dtype)


def _mla_pre(h, mod, modkv, g_pre, g_kv_in, w_dkv, g_kv_lat, ck, sk, w_dq, g_q, w_uq, w_ukt, cq, sq, bb, tt,
             q_dtype):
    b, t, d = h.shape
    q_rank = w_dq.shape[1]
    tok = lambda n: pl.BlockSpec((bb, tt, n), lambda i, j: (i, j, 0))
    head = lambda n: pl.BlockSpec((bb, N_HEADS, tt, n), lambda i, j: (i, 0, j, 0))
    seq = lambda n: pl.BlockSpec((bb, n, d), lambda i, j: (i, 0, 0))
    pos = lambda n: pl.BlockSpec((tt, n), lambda i, j: (j, 0))
    kern = functools.partial(_mla_pre_kernel, bb=bb, tt=tt, d=d)
    return pl.pallas_call(
        kern,
        out_shape=[jax.ShapeDtypeStruct((b, t, KV_RANK), F32), jax.ShapeDtypeStruct((b, t, ROPE_DIM), F32),
                   jax.ShapeDtypeStruct((b, N_HEADS, t, KV_RANK), q_dtype),
                   jax.ShapeDtypeStruct((b, N_HEADS, t, ROPE_DIM), q_dtype)],
        grid=(b // bb, t // tt),
        in_specs=[tok(d), seq(6), seq(2), _const_spec((1, d)), _const_spec((1, d)),
                  _const_spec(w_dkv.shape), _const_spec((1, KV_RANK)), pos(ROPE_DIM), pos(ROPE_DIM),
                  _const_spec(w_dq.shape), _const_spec((1, q_rank)), _const_spec(w_uq.shape),
                  _const_spec(w_ukt.shape), pos(N_HEADS * ROPE_DIM), pos(N_HEADS * ROPE_DIM)],
        out_specs=[tok(KV_RANK), tok(ROPE_DIM), head(KV_RANK), head(ROPE_DIM)],
        compiler_params=_cparams(("arbitrary", "arbitrary")),
        name="mla_pre",
    )(h, mod, modkv, g_pre, g_kv_in, w_dkv, g_kv_lat, ck, sk, w_dq, g_q, w_uq, w_ukt, cq, sq)


LANES = 128


def _softmax_step(s, m_prev, l_prev):
    keys = s.shape[1]
    m_new = jnp.maximum(m_prev, jnp.max(s, axis=-1, keepdims=True))
    alpha = jnp.exp(m_prev - m_new)
    p = jnp.exp(s - (jnp.tile(m_new, (1, keys // LANES)) if keys >= LANES else m_new[:, :keys]))
    l_new = alpha * l_prev + jnp.sum(p, axis=-1, keepdims=True)
    return p, m_new, l_new, alpha


def _attn_prompt_kernel(ql_ref, qr_ref, ckv_ref, kr_ref, o_ref, m_ref, l_ref, acc_ref, *, tq):
    i, j = pl.program_id(1), pl.program_id(2)
    rows = N_HEADS * tq

    @pl.when(j == 0)
    def _():
        m_ref[...] = jnp.full_like(m_ref, -jnp.inf)
        l_ref[...] = jnp.zeros_like(l_ref)
        acc_ref[...] = jnp.zeros_like(acc_ref)

    def step(masked):
        ckv = ckv_ref[0].astype(BF16)
        kr = kr_ref[0].astype(BF16)
        q = ql_ref[0].reshape(rows, KV_RANK)
        qr = qr_ref[0].reshape(rows, ROPE_DIM)
        s = (_dot_nt(q, ckv) + _dot_nt(qr, kr)) * SM_SCALE
        if masked:
            q_pos = lax.broadcasted_iota(jnp.int32, (rows, tq), 0) % tq
            k_pos = lax.broadcasted_iota(jnp.int32, (rows, tq), 1)
            s = jnp.where(k_pos <= q_pos, s, NEG_INF)
        p, m_new, l_new, alpha = _softmax_step(s, m_ref[...], l_ref[...])
        m_ref[...] = m_new
        l_ref[...] = l_new
        acc_ref[...] = (jnp.tile(alpha, (1, KV_RANK // LANES)) * acc_ref[...]
                        + jnp.dot(p.astype(BF16), ckv, preferred_element_type=F32))

    @pl.when(j < i)
    def _():
        step(False)

    @pl.when(j == i)
    def _():
        step(True)
        out = acc_ref[...] / jnp.tile(l_ref[...], (1, KV_RANK // LANES))
        o_ref[0] = out.reshape(N_HEADS, tq, KV_RANK).astype(o_ref.dtype)


def _attn_prompt(q_lat, q_rope, c_kv, k_rope, tq):
    b, _, t, _ = q_lat.shape
    n = t // tq
    rows = N_HEADS * tq
    qspec = lambda w: pl.BlockSpec((1, N_HEADS, tq, w), lambda bi, i, j: (bi, 0, i, 0))
    kspec = lambda w: pl.BlockSpec((1, tq, w), lambda bi, i, j: (bi, jnp.minimum(i, j), 0))
    return pl.pallas_call(
        functools.partial(_attn_prompt_kernel, tq=tq),
        out_shape=jax.ShapeDtypeStruct((b, N_HEADS, t, KV_RANK), BF16),
        grid=(b, n, n),
        in_specs=[qspec(KV_RANK), qspec(ROPE_DIM), kspec(KV_RANK), kspec(ROPE_DIM)],
        out_specs=qspec(KV_RANK),
        scratch_shapes=[pltpu.VMEM((rows, LANES), F32), pltpu.VMEM((rows, LANES), F32),
                        pltpu.VMEM((rows, KV_RANK), F32)],
        compiler_params=_cparams(("arbitrary", "arbitrary", "arbitrary")),
        name="attn_prompt",
    )(q_lat, q_rope, c_kv, k_rope)


def _attn_paged_kernel(pt_ref, ql_ref, qr_ref, cnew_ref, krnew_ref, *rest, ppc, spb, t_new):
    lat_pages = rest[:spb * ppc]
    rope_pages = rest[spb * ppc:2 * spb * ppc]
    o_ref = rest[2 * spb * ppc]
    kbuf, m_ref, l_ref, acc_ref = rest[2 * spb * ppc + 1:]
    j = pl.program_id(1)
    rows = N_HEADS * t_new

    @pl.when(j == 0)
    def _():
        m_ref[...] = jnp.full_like(m_ref, -jnp.inf)
        l_ref[...] = jnp.zeros_like(l_ref)
        acc_ref[...] = jnp.zeros_like(acc_ref)

    for sq in range(spb):
        q = ql_ref[sq].reshape(rows, KV_RANK).astype(BF16)
        qr = qr_ref[sq].reshape(rows, ROPE_DIM).astype(BF16)
        s_pages = []
        for k in range(ppc):
            kb = lat_pages[sq * ppc + k][0].astype(BF16)
            kbuf[sq, k * PAGE_SIZE:(k + 1) * PAGE_SIZE, :] = kb
            s_pages.append(_dot_nt(q, kb) + _dot(qr, rope_pages[sq * ppc + k][0]))
        s = jnp.concatenate(s_pages, axis=1) * SM_SCALE
        p, m_new, l_new, alpha = _softmax_step(s, m_ref[sq], l_ref[sq])
        m_ref[sq] = m_new
        l_ref[sq] = l_new
        acc_ref[sq] = (jnp.tile(alpha, (1, KV_RANK // LANES)) * acc_ref[sq]
                       + jnp.dot(p.astype(BF16), kbuf[sq], preferred_element_type=F32))

    @pl.when(j == pl.num_programs(1) - 1)
    def _():
        for sq in range(spb):
            q = ql_ref[sq].reshape(rows, KV_RANK).astype(BF16)
            qr = qr_ref[sq].reshape(rows, ROPE_DIM).astype(BF16)
            c_new = cnew_ref[sq].astype(BF16)
            s_new = (_dot_nt(q, c_new) + _dot_nt(qr, krnew_ref[sq])) * SM_SCALE
            q_pos = lax.broadcasted_iota(jnp.int32, (rows, t_new), 0) % t_new
            k_pos = lax.broadcasted_iota(jnp.int32, (rows, t_new), 1)
            s_new = jnp.where(k_pos <= q_pos, s_new, NEG_INF)
            p, m_new, l_new, alpha = _softmax_step(s_new, m_ref[sq], l_ref[sq])
            acc = (jnp.tile(alpha, (1, KV_RANK // LANES)) * acc_ref[sq]
                   + jnp.dot(p.astype(BF16), c_new, preferred_element_type=F32))
            out = acc / jnp.tile(l_new, (1, KV_RANK // LANES))
            o_ref[sq] = out.reshape(N_HEADS, t_new, KV_RANK)


def _attn_paged(q_lat, q_rope, c_new, kr_new, cache_lat, cache_rope_t, page_table, ppc, spb):
    s, _, t_new, _ = q_lat.shape
    n_pages = page_table.shape[1]
    assert n_pages % ppc == 0 and s % spb == 0
    rows = N_HEADS * t_new
    head = lambda w: pl.BlockSpec((spb, N_HEADS, t_new, w), lambda b, j, pt: (b, 0, 0, 0))
    seq = lambda w: pl.BlockSpec((spb, t_new, w), lambda b, j, pt: (b, 0, 0))
    lat_page = lambda sq, k: pl.BlockSpec((1, PAGE_SIZE, KV_RANK),
                                          lambda b, j, pt: (pt[b * spb + sq, j * ppc + k], 0, 0))
    rope_page = lambda sq, k: pl.BlockSpec((1, ROPE_DIM, PAGE_SIZE),
                                           lambda b, j, pt: (pt[b * spb + sq, j * ppc + k], 0, 0))
    pairs = [(sq, k) for sq in range(spb) for k in range(ppc)]
    grid_spec = pltpu.PrefetchScalarGridSpec(
        num_scalar_prefetch=1,
        grid=(s // spb, n_pages // ppc),
        in_specs=[head(KV_RANK), head(ROPE_DIM), seq(KV_RANK), seq(ROPE_DIM)]
                 + [lat_page(sq, k) for sq, k in pairs] + [rope_page(sq, k) for sq, k in pairs],
        out_specs=head(KV_RANK),
        scratch_shapes=[pltpu.VMEM((spb, ppc * PAGE_SIZE, KV_RANK), BF16),
                        pltpu.VMEM((spb, rows, LANES), F32), pltpu.VMEM((spb, rows, LANES), F32),
                        pltpu.VMEM((spb, rows, KV_RANK), F32)],
    )
    n_in = spb * ppc
    return pl.pallas_call(
        functools.partial(_attn_paged_kernel, ppc=ppc, spb=spb, t_new=t_new),
        out_shape=jax.ShapeDtypeStruct((s, N_HEADS, t_new, KV_RANK), F32),
        grid_spec=grid_spec,
        compiler_params=_cparams(("arbitrary", "arbitrary")),
        name="attn_paged",
    )(page_table, q_lat, q_rope, c_new, kr_new, *([cache_lat] * n_in), *([cache_rope_t] * n_in))


def _mla_post_kernel(h_ref, ol_ref, mod_ref, w_uv, w_o, g_post, g_pre_ffn, w_router, b_router,
                     h_out, f_out, e_out, g_out, c_out, obuf, cnt_ref, *, bb, tt, d):
    rows = bb * tt
    i, j = pl.program_id(0), pl.program_id(1)
    h = h_ref[...].reshape(rows, d)
    for hd in range(N_HEADS):
        o_lat = ol_ref[:, hd].reshape(rows, KV_RANK)
        obuf[:, hd * V_DIM:(hd + 1) * V_DIM] = _dot(o_lat, w_uv[hd]).astype(BF16)
    mix = jnp.dot(obuf[...], w_o[...], preferred_element_type=F32)
    mv = lambda k: _modvec(mod_ref, k, bb, tt)
    _stage_tail(jnp.logical_and(i == 0, j == 0), h, mix, mv(2), mv(3), mv(4), g_post[...], g_pre_ffn[...],
                w_router[...], b_router[...], cnt_ref, h_out, f_out, e_out, g_out, c_out, bb, tt)


def _mla_post(h, o_lat, mod, w_uv, w_o, g_post, g_pre_ffn, w_router, b_router, bb, tt):
    b, t, d = h.shape
    tok = lambda n: pl.BlockSpec((bb, tt, n), lambda i, j: (i, j, 0))
    kern = functools.partial(_mla_post_kernel, bb=bb, tt=tt, d=d)
    return pl.pallas_call(
        kern,
        out_shape=_tail_out_shapes(b, t, d),
        grid=(b // bb, t // tt),
        in_specs=[tok(d), pl.BlockSpec((bb, N_HEADS, tt, KV_RANK), lambda i, j: (i, 0, j, 0)),
                  pl.BlockSpec((bb, 6, d), lambda i, j: (i, 0, 0)),
                  _const_spec(w_uv.shape), _const_spec(w_o.shape), _const_spec((1, d)), _const_spec((1, d)),
                  _const_spec((d, N_EXPERTS)), _const_spec((1, N_EXPERTS))],
        out_specs=_tail_out_specs(bb, tt, d),
        scratch_shapes=[pltpu.VMEM((bb * tt, N_HEADS * V_DIM), BF16), pltpu.VMEM((1, N_EXPERTS), F32)],
        compiler_params=_cparams(("arbitrary", "arbitrary")),
        name="mla_post",
    )(h, o_lat, mod, w_uv, w_o, g_post, g_pre_ffn, w_router, b_router)


def _moe_layer(f, top_e, gates, counts, h1, mod, g_post, layer, wgu, bgu, wd, bd, bm, bb, tt):
    d = f[0].shape[-1]
    n_tok = [x.shape[0] * x.shape[1] for x in f]
    starts = [sum(n_tok[:g]) for g in range(len(f))]
    n_real = sum(n_tok)
    n_assign = n_real * TOP_K
    n_blocks = (n_assign + N_EXPERTS * (bm - 1) + bm - 1) // bm
    counts = sum(c.reshape(N_EXPERTS).astype(jnp.int32) for c in counts)
    padded = (counts + bm - 1) // bm * bm
    pad_end = jnp.cumsum(padded)
    pad_start = pad_end - padded
    sorted_start = jnp.cumsum(counts) - counts
    order = jnp.argsort(jnp.concatenate([e.reshape(-1) for e in top_e]), stable=True).astype(jnp.int32)

    n_dummy_tok = RING * bm // TOP_K
    plane = n_real + n_dummy_tok
    tok_bits = (plane - 1).bit_length()
    rows = jnp.arange(n_blocks + LEAD_BLOCKS, dtype=jnp.int32)[:, None]
    cols = jnp.arange(bm, dtype=jnp.int32)[None, :]
    dummy = (rows % RING) * bm + cols
    dummy = ((dummy % TOP_K) << tok_bits) + n_real + dummy // TOP_K
    block_start = jnp.arange(n_blocks, dtype=jnp.int32) * bm
    block_e = jnp.minimum(jnp.sum(pad_end[None, :] <= block_start[:, None], axis=1), N_EXPERTS - 1).astype(jnp.int32)
    within = block_start[:, None] + cols - pad_start[block_e][:, None]
    valid = within < counts[block_e][:, None]
    a = order[jnp.clip(sorted_start[block_e][:, None] + within, 0, n_assign - 1)]
    real = ((a % TOP_K) << tok_bits) + a // TOP_K
    asg = jnp.concatenate([dummy[:LEAD_BLOCKS], jnp.where(valid, real, dummy[LEAD_BLOCKS:])], axis=0)

    x2d = jnp.concatenate([x.reshape(n, d) for x, n in zip(f, n_tok)] + [jnp.zeros((n_dummy_tok, d), F32)], axis=0)
    y = _moe(x2d, asg, block_e, tok_bits, layer, wgu, bgu, wd, bd, bm)
    y_tok = y.reshape(TOP_K, plane, d)
    return [_combine(y_tok, starts[g], gates[g], h1[g], mod[g], g_post, bb[g], tt[g]) for g in range(len(f))]


def _rope_tables(pos):
    inv = jnp.exp(-jnp.log(ROPE_THETA) * jnp.arange(HALF_ROPE, dtype=jnp.float32) / HALF_ROPE)
    ang = pos.astype(jnp.float32)[:, None] * inv[None, :]
    cos, sin = jnp.cos(ang), jnp.sin(ang)
    ck = jnp.concatenate([cos, cos], axis=-1)
    sk = jnp.concatenate([-sin, sin], axis=-1)
    return ck, sk, jnp.tile(ck, (1, N_HEADS)), jnp.tile(sk, (1, N_HEADS))


def _trunk(groups, w, bm):
    row = lambda v: v.reshape(1, -1)
    col = lambda outs, k: [o[k] for o in outs]

    def moe(stage, layer):
        return _moe_layer(col(stage, 1), col(stage, 2), col(stage, 3), col(stage, 4), col(stage, 0),
                          [g['mods'][layer] for g in groups], row(w['g_post_ffn'][layer]), layer,
                          w['w_gate_up'], w['b_gate_up'], w['w_down'], w['b_down'], bm,
                          [g['tiles']['cbb'] for g in groups], [g['tiles']['ctt'] for g in groups])

    stage = [_conv_stage(
        g['x'], g['mods'][0], g['prev'], row(w['g_pre_mix'][0]), w['w_conv_in'], w['w_conv_dw'], w['w_conv_out'],
        row(w['g_post_mix'][0]), row(w['g_pre_ffn'][0]), w['w_router'][0], row(w['b_router'][0]),
        g['tiles']['bb'], g['tiles']['tt']) for g in groups]
    u_tails = col(stage, 5)
    hs = moe(stage, 0)

    stage, lat, rope = [], [], []
    for g, h in zip(groups, hs):
        t = g['tiles']
        ck, sk, cq, sq = _rope_tables(g['pos'])
        c_kv, k_rope, q_lat, q_rope = _mla_pre(
            h, g['mods'][1], g['modkv'], row(w['g_pre_mix'][1]), row(w['g_kv_in']), w['w_dkv'], row(w['g_kv_lat']),
            ck, sk, w['w_dq'], row(w['g_q']), w['w_uq'], w['w_ukt'], cq, sq, t['bb'], t['tt'], t['q_dtype'])
        o_lat = g['attend'](q_lat, q_rope, c_kv, k_rope)
        stage.append(_mla_post(h, o_lat, g['mods'][1], w['w_uv'], w['w_o'], row(w['g_post_mix'][1]),
                               row(w['g_pre_ffn'][1]), w['w_router'][1], row(w['b_router'][1]), t['bb'], t['tt']))
        lat.append(c_kv)
        rope.append(k_rope)
    ys = moe(stage, 1)
    conv = [u[None, :, TAIL_ROWS - (CONV_W - 1):, :] for u in u_tails]
    return ys, conv, lat, rope


def _prepare_weights(p):
    w = dict(p)
    w['w_conv_in'] = p['w_conv_in'][0].astype(BF16)
    w['w_conv_dw'] = p['w_conv_dw'][0]
    w['w_conv_out'] = p['w_conv_out'][0].astype(BF16)
    wk = p['w_dkv']
    w['w_dkv'] = jnp.concatenate([wk, wk[:, KV_RANK + HALF_ROPE:], wk[:, KV_RANK:KV_RANK + HALF_ROPE]],
                                 axis=1).astype(BF16)
    wq = p['w_uq'][0].reshape(-1, N_HEADS, NOPE_DIM + ROPE_DIM)
    q_rank = wq.shape[0]
    nope = wq[:, :, :NOPE_DIM].reshape(q_rank, N_HEADS * NOPE_DIM)
    rope = wq[:, :, NOPE_DIM:]
    rope_sw = jnp.concatenate([rope[:, :, HALF_ROPE:], rope[:, :, :HALF_ROPE]], axis=-1)
    w['w_uq'] = jnp.concatenate([nope, rope.reshape(q_rank, -1), rope_sw.reshape(q_rank, -1)], axis=1).astype(BF16)
    w['w_dq'] = p['w_dq'][0].astype(BF16)
    w['g_q'] = p['g_q'][0]
    w['w_ukt'] = jnp.transpose(p['w_uk'], (1, 2, 0)).astype(BF16)
    w['w_uv'] = jnp.transpose(p['w_uv'], (1, 0, 2)).astype(BF16)
    w['w_o'] = p['w_o'][0].astype(BF16)
    w['w_router'] = p['w_router'].astype(BF16)
    w['b_gate_up'] = p['b_gate_up'][:, :, None, :]
    w['b_down'] = p['b_down'][:, :, None, :]
    return w


def kernel(x_prompt, x_sample, cache_kv_latent, cache_k_rope, state_conv, page_table, c_prompt, c_sample,
           w_ada, b_ada, g_pre_mix, g_post_mix, g_pre_ffn, g_post_ffn, w_conv_in, w_conv_dw, w_conv_out,
           w_ada_kv, b_ada_kv, g_kv_in, w_dkv, g_kv_lat, w_uk, w_uv, w_dq, g_q, w_uq, w_o,
           w_router, b_router, w_gate_up, b_gate_up, w_down, b_down):
    p = dict(g_pre_mix=g_pre_mix, g_post_mix=g_post_mix, g_pre_ffn=g_pre_ffn, g_post_ffn=g_post_ffn,
             w_conv_in=w_conv_in, w_conv_dw=w_conv_dw, w_conv_out=w_conv_out, g_kv_in=g_kv_in, w_dkv=w_dkv,
             g_kv_lat=g_kv_lat, w_uk=w_uk, w_uv=w_uv, w_dq=w_dq, g_q=g_q, w_uq=w_uq, w_o=w_o,
             w_router=w_router, b_router=b_router, w_gate_up=w_gate_up, b_gate_up=b_gate_up,
             w_down=w_down, b_down=b_down)
    w = _prepare_weights(p)
    bp, tp, d = x_prompt.shape
    bs, ts, _ = x_sample.shape
    depth = w_ada.shape[0]

    c_all = jnp.concatenate([c_prompt, c_sample], axis=0)
    mods = [_ada(c_all, w_ada[l], b_ada[l]).reshape(bp + bs, 6, d) for l in range(depth)]
    modkv = _ada(c_all, w_ada_kv, b_ada_kv).reshape(bp + bs, 2, d)

    pos_p = jnp.arange(tp, dtype=jnp.int32)
    prev_p = jnp.zeros((bp, TAIL_ROWS, d), F32)
    tiles_p = dict(bb=1, tt=min(tp, 512), cbb=1, ctt=min(tp, 256), q_dtype=BF16)
    attend_p = lambda ql, qr, ckv, kr: _attn_prompt(ql, qr, ckv, kr, min(tp, 512))
    prompt = dict(x=x_prompt, mods=[m[:bp] for m in mods], modkv=modkv[:bp], pos=pos_p, prev=prev_p,
                  attend=attend_p, tiles=tiles_p)

    n_pages = page_table.shape[1]
    pos_s = n_pages * PAGE_SIZE + jnp.arange(ts, dtype=jnp.int32)
    prev_s = jnp.concatenate([jnp.zeros((bs, TAIL_ROWS - (CONV_W - 1), d), F32), state_conv[0]], axis=1)
    sbb = min(bs, 32)
    tiles_s = dict(bb=sbb, tt=ts, cbb=sbb, ctt=ts, q_dtype=F32)
    ppc = math.gcd(n_pages, 16)
    spb = math.gcd(bs, 2)
    cache_rope_t = jnp.swapaxes(cache_k_rope, 1, 2)
    attend_s = lambda ql, qr, ckv, kr: _attn_paged(ql, qr, ckv, kr, cache_kv_latent, cache_rope_t, page_table,
                                                   ppc, spb)
    sample = dict(x=x_sample, mods=[m[bp:] for m in mods], modkv=modkv[bp:], pos=pos_s, prev=prev_s,
                  attend=attend_s, tiles=tiles_s)

    (y_p, y_s), (conv_p, conv_s), (lat_p, lat_s), (rope_p, rope_s) = _trunk([prompt, sample], w, MOE_BLOCK_ROWS)
    return (y_p, y_s, lat_p, rope_p, conv_p, lat_s, rope_s, conv_s)
```
